```python
import math
import jax, jax.numpy as jnp
from jax import lax
import numpy as np

D_MODEL = 1024
BATCH = 2
SEQ = 8192
DEPTH = 4

N_A_LAYERS = DEPTH // 2
N_B_LAYERS = DEPTH - N_A_LAYERS
RET_HEADS = 4
RET_QK_DIM = D_MODEL // RET_HEADS
RET_V_DIM = 2 * D_MODEL // RET_HEADS
RET_CHUNK = 128
ROPE_BASE = 10000.0
SWA_HEADS = 16
SWA_KV_HEADS = 4
SWA_GROUP = SWA_HEADS // SWA_KV_HEADS
SWA_HEAD_DIM = 64
WINDOW = 128
BLOCK = WINDOW
REL_BUCKETS = 32
REL_MAX_DIST = 128
D_FF = 2816
CONV_WIDTH = 3
NORM_EPS = 1e-6

kernel_name = "yoco_retention_swa_sink_hybrid"


def rmsnorm(x, g):
    x32 = x.astype(jnp.float32)
    y = x32 * lax.rsqrt(jnp.mean(x32 * x32, axis=-1, keepdims=True) + NORM_EPS)
    return (y * g.astype(jnp.float32)).astype(x.dtype)


def adaln_in(x, g, shift, scale):
    return rmsnorm(x, g) * (1 + scale[:, None, :]) + shift[:, None, :]


def rotary(x, pos):
    d = x.shape[-1]
    half = d // 2
    inv = ROPE_BASE ** (-jnp.arange(half, dtype=jnp.float32) / half)
    ang = pos.astype(jnp.float32)[:, None] * inv[None, :]
    cos = jnp.cos(ang)[None, :, None, :]
    sin = jnp.sin(ang)[None, :, None, :]
    x32 = x.astype(jnp.float32)
    x1, x2 = x32[..., :half], x32[..., half:]
    return jnp.concatenate([x1 * cos - x2 * sin, x2 * cos + x1 * sin], axis=-1)


def retention(h, w_in, w_out):
    B, S, _ = h.shape
    H, dk, dv, C = RET_HEADS, RET_QK_DIM, RET_V_DIM, RET_CHUNK
    NC = S // C
    proj = h @ w_in
    q, k, v, g = jnp.split(proj, [H * dk, 2 * H * dk, 2 * H * dk + H * dv], axis=-1)
    pos = jnp.arange(S)
    q = rotary(q.reshape(B, S, H, dk), pos)
    k = rotary(k.reshape(B, S, H, dk), pos) * (dk ** -0.5)
    v = v.reshape(B, S, H, dv).astype(jnp.float32)

    def to_chunks(t):
        return t.reshape(B, NC, C, H, t.shape[-1]).transpose(1, 0, 3, 2, 4)

    log_gamma = jnp.log(1.0 - 2.0 ** (-5.0 - jnp.arange(H, dtype=jnp.float32)))
    idx = jnp.arange(C, dtype=jnp.float32)
    diff = idx[:, None] - idx[None, :]
    dmask = jnp.where(diff[None] >= 0, jnp.exp(jnp.maximum(diff, 0.0)[None] * log_gamma[:, None, None]), 0.0)
    xi = jnp.exp((idx[None, :] + 1.0) * log_gamma[:, None])
    zeta = jnp.exp((C - 1.0 - idx[None, :]) * log_gamma[:, None])
    g_chunk = jnp.exp(C * log_gamma)

    def step(R, qkv):
        qc, kc, vc = qkv
        s = jnp.einsum('bhqd,bhkd->bhqk', qc, kc) * dmask[None]
        inner = jnp.einsum('bhqk,bhkv->bhqv', s, vc)
        cross = jnp.einsum('bhqd,bhdv->bhqv', qc, R) * xi[None, :, :, None]
        R_new = R * g_chunk[None, :, None, None] + jnp.einsum('bhkd,bhkv->bhdv', kc, vc * zeta[None, :, :, None])
        return R_new, inner + cross

    R0 = jnp.zeros((B, H, dk, dv), jnp.float32)
    _, o = lax.scan(step, R0, (to_chunks(q), to_chunks(k), to_chunks(v)))
    o = o.transpose(1, 0, 3, 2, 4).reshape(B, S, H, dv)
    mu = jnp.mean(o, axis=-1, keepdims=True)
    var = jnp.mean(jnp.square(o - mu), axis=-1, keepdims=True)
    o = ((o - mu) * lax.rsqrt(var + NORM_EPS)).reshape(B, S, H * dv).astype(h.dtype)
    return (jax.nn.silu(g) * o) @ w_out


def shared_kv(x, g, shift, scale, w_kv):
    B, S, _ = x.shape
    NB = S // BLOCK
    h = adaln_in(x, g, shift, scale)
    k, v = jnp.split(h @ w_kv, 2, axis=-1)

    def band(t):
        t = t.reshape(B, S, SWA_KV_HEADS, SWA_HEAD_DIM)
        t = jnp.pad(t, ((0, 0), (BLOCK, 0), (0, 0), (0, 0)))
        tb = t.reshape(B, NB + 1, BLOCK, SWA_KV_HEADS, SWA_HEAD_DIM)
        return jnp.concatenate([tb[:, :-1], tb[:, 1:]], axis=2)

    return band(k), band(v)


def rel_bias_band(table):
    i = jnp.arange(BLOCK)[:, None]
    j = jnp.arange(2 * BLOCK)[None, :]
    dist = i + BLOCK - j
    n = jnp.maximum(dist, 0)
    max_exact = REL_BUCKETS // 2
    large = max_exact + (jnp.log(jnp.maximum(n, 1).astype(jnp.float32) / max_exact)
                         / math.log(REL_MAX_DIST / max_exact) * (REL_BUCKETS - max_exact)).astype(jnp.int32)
    large = jnp.minimum(large, REL_BUCKETS - 1)
    bucket = jnp.where(n < max_exact, n, large)
    bias = table[bucket].astype(jnp.float32)
    bias = bias.transpose(2, 0, 1).reshape(SWA_KV_HEADS, SWA_GROUP, BLOCK, 2 * BLOCK)
    win = (dist >= 0) & (dist < WINDOW)
    return bias, win


def swa_sink_attention(h, w_q, w_o, sinks, k_band, v_band, bias, mask):
    B, S, _ = h.shape
    NB = S // BLOCK
    q = (h @ w_q).reshape(B, NB, BLOCK, SWA_KV_HEADS, SWA_GROUP, SWA_HEAD_DIM)
    s = jnp.einsum('bnqhgd,bnkhd->bnhgqk', q, k_band).astype(jnp.float32) * (SWA_HEAD_DIM ** -0.5)
    s = s + bias[None, None]
    s = jnp.where(mask[None, :, None, None], s, -jnp.inf)
    sink = sinks.astype(jnp.float32).reshape(SWA_KV_HEADS, SWA_GROUP)[None, None, :, :, None, None]
    m = jnp.maximum(jnp.max(s, axis=-1, keepdims=True), sink)
    e = jnp.exp(s - m)
    p = e / (jnp.sum(e, axis=-1, keepdims=True) + jnp.exp(sink - m))
    o = jnp.einsum('bnhgqk,bnkhd->bnqhgd', p.astype(v_band.dtype), v_band)
    return o.reshape(B, S, SWA_HEADS * SWA_HEAD_DIM) @ w_o


def conv_ffn(h, w_up, conv_w, conv_b, w_down):
    S = h.shape[1]
    u = h @ w_up
    u_pad = jnp.pad(u, ((0, 0), (CONV_WIDTH - 1, 0), (0, 0)))
    uc = conv_b + sum(conv_w[j] * u_pad[:, j:j + S] for j in range(CONV_WIDTH))
    gate, val = jnp.split(uc, 2, axis=-1)
    return (jax.nn.silu(gate) * val) @ w_down


def setup_inputs(seed: int = 0) -> dict:
    key = jax.random.key(seed)
    ks = jax.random.split(key, 24)
    D, F = D_MODEL, D_FF
    nrm = lambda k, shape, s: jax.random.normal(k, shape, jnp.float32) * s
    ret_in_cols = 2 * RET_HEADS * RET_QK_DIM + 2 * RET_HEADS * RET_V_DIM
    return {
        "x": nrm(ks[0], (BATCH, SEQ, D), 1.0),
        "c": nrm(ks[1], (BATCH, D), 1.0),
        "norm_g": 1.0 + nrm(ks[2], (DEPTH, 4, D), 0.05),
        "ada_w": nrm(ks[3], (DEPTH, D, 6 * D), 0.1 * D ** -0.5),
        "ada_b": nrm(ks[4], (DEPTH, 6 * D), 0.01),
        "ret_w_in": nrm(ks[5], (N_A_LAYERS, D, ret_in_cols), D ** -0.5),
        "ret_w_out": nrm(ks[6], (N_A_LAYERS, RET_HEADS * RET_V_DIM, D), (RET_HEADS * RET_V_DIM) ** -0.5),
        "kv_norm_g": 1.0 + nrm(ks[7], (D,), 0.05),
        "kv_ada_w": nrm(ks[8], (D, 2 * D), 0.1 * D ** -0.5),
        "kv_ada_b": nrm(ks[9], (2 * D,), 0.01),
        "kv_w": nrm(ks[10], (D, 2 * SWA_KV_HEADS * SWA_HEAD_DIM), D ** -0.5),
        "swa_w_q": nrm(ks[11], (N_B_LAYERS, D, SWA_HEADS * SWA_HEAD_DIM), D ** -0.5),
        "swa_w_o": nrm(ks[12], (N_B_LAYERS, SWA_HEADS * SWA_HEAD_DIM, D), (SWA_HEADS * SWA_HEAD_DIM) ** -0.5),
        "swa_sinks": nrm(ks[13], (N_B_LAYERS, SWA_HEADS), 0.5),
        "rel_bias": nrm(ks[14], (REL_BUCKETS, SWA_HEADS), 0.5),
        "ffn_w_up": nrm(ks[15], (DEPTH, D, 2 * F), D ** -0.5),
        "ffn_conv_w": nrm(ks[16], (DEPTH, CONV_WIDTH, 2 * F), CONV_WIDTH ** -0.5),
        "ffn_conv_b": nrm(ks[17], (DEPTH, 2 * F), 0.01),
        "ffn_w_down": nrm(ks[18], (DEPTH, F, D), F ** -0.5),
    }


def reference(x, c, norm_g, ada_w, ada_b, ret_w_in, ret_w_out, kv_norm_g, kv_ada_w, kv_ada_b, kv_w,
              swa_w_q, swa_w_o, swa_sinks, rel_bias, ffn_w_up, ffn_conv_w, ffn_conv_b, ffn_w_down):
    S = x.shape[1]
    NB = S // BLOCK
    c_act = jax.nn.silu(c)
    bias, win = rel_bias_band(rel_bias)
    kpos = jnp.arange(NB)[:, None] * BLOCK - BLOCK + jnp.arange(2 * BLOCK)[None, :]
    mask = win[None] & (kpos >= 0)[:, None, :]
    k_band = None
    v_band = None
    for l in range(DEPTH):
        mod = c_act @ ada_w[l] + ada_b[l]
        sh_m, sc_m, gt_m, sh_f, sc_f, gt_f = jnp.split(mod, 6, axis=-1)
        h = adaln_in(x, norm_g[l, 0], sh_m, sc_m)
        if l < N_A_LAYERS:
            y = retention(h, ret_w_in[l], ret_w_out[l])
        else:
            i = l - N_A_LAYERS
            y = swa_sink_attention(h, swa_w_q[i], swa_w_o[i], swa_sinks[i], k_band, v_band, bias, mask)
        x = x + (1 + gt_m)[:, None, :] * rmsnorm(y, norm_g[l, 1])
        h = adaln_in(x, norm_g[l, 2], sh_f, sc_f)
        y = conv_ffn(h, ffn_w_up[l], ffn_conv_w[l], ffn_conv_b[l], ffn_w_down[l])
        x = x + (1 + gt_f)[:, None, :] * rmsnorm(y, norm_g[l, 3])
        if l == N_A_LAYERS - 1:
            kv_mod = c_act @ kv_ada_w + kv_ada_b
            kv_shift, kv_scale = jnp.split(kv_mod, 2, axis=-1)
            k_band, v_band = shared_kv(x, kv_norm_g, kv_shift, kv_scale, kv_w)
    return x
```

```python
import functools
import math

import jax
import jax.numpy as jnp
import numpy as np
from jax import lax
from jax.experimental import pallas as pl
from jax.experimental.pallas import tpu as pltpu

D_MODEL = 1024
DEPTH = 4
N_A_LAYERS = DEPTH // 2
RET_HEADS = 4
RET_QK_DIM = D_MODEL // RET_HEADS
RET_V_DIM = 2 * D_MODEL // RET_HEADS
RET_CHUNK = 128
ROPE_BASE = 10000.0
SWA_HEADS = 16
SWA_KV_HEADS = 4
SWA_GROUP = SWA_HEADS // SWA_KV_HEADS
SWA_HEAD_DIM = 64
WINDOW = 128
BLOCK = WINDOW
REL_BUCKETS = 32
REL_MAX_DIST = 128
D_FF = 2816
CONV_WIDTH = 3
NORM_EPS = 1e-6

V7X_SUBLANES = 8
V7X_LANES = 128
V7X_MXU_DIM = 256
V7X_VMEM_LIMIT_BYTES = 58 * 1024 * 1024

MASK_NEG = -1e30
BF16 = jnp.bfloat16
F32 = jnp.float32

FFN_COL_CHUNK = V7X_MXU_DIM
GQ_ROWS = SWA_GROUP * BLOCK
GQ_LANES = SWA_GROUP * SWA_HEAD_DIM


def _dot(a, b):
    return jnp.dot(a, b, preferred_element_type=F32)


def _sigmoid(x):
    return 1.0 / (1.0 + jnp.exp(-x))


def _adaln(x, g, shift, scale):
    ms = jnp.mean(x * x, axis=-1, keepdims=True)
    y = x * lax.rsqrt(ms + NORM_EPS) * g
    return y * (1.0 + scale) + shift


def _rmsnorm(y, g):
    ms = jnp.mean(y * y, axis=-1, keepdims=True)
    return y * lax.rsqrt(ms + NORM_EPS) * g


def _resident(shape):
    zeros = (0,) * len(shape)
    return pl.BlockSpec(shape, lambda b, i: zeros, pipeline_mode=pl.Buffered(1))


def _params():
    return pltpu.CompilerParams(
        dimension_semantics=("arbitrary", "arbitrary"),
        vmem_limit_bytes=V7X_VMEM_LIMIT_BYTES)


def _mod_kernel(c_ref, w_ref, b_ref, o_ref):
    c = c_ref[...]
    ca = (c * _sigmoid(c)).astype(BF16)
    o_ref[0] = _dot(ca, w_ref[0].astype(BF16)) + b_ref[0]


def _modulation(c_pad, w, b, tn):
    L, Dm, N = w.shape
    return pl.pallas_call(
        _mod_kernel,
        grid=(L, N // tn),
        in_specs=[
            pl.BlockSpec((V7X_SUBLANES, Dm), lambda l, j: (0, 0)),
            pl.BlockSpec((1, Dm, tn), lambda l, j: (l, 0, j)),
            pl.BlockSpec((1, 1, tn), lambda l, j: (l, 0, j)),
        ],
        out_specs=pl.BlockSpec((1, V7X_SUBLANES, tn), lambda l, j: (l, 0, j)),
        out_shape=jax.ShapeDtypeStruct((L, V7X_SUBLANES, N), F32),
        compiler_params=_params(),
        name="adaln_modulation",
    )(c_pad, w, b)


def _ffn_kernel(x_ref, mod_ref, g_ref, wup_ref, cw_ref, cb_ref, wdn_ref, o_ref,
                u_ref, act_ref, carry_ref, *, tm):
    F = D_FF
    FC = FFN_COL_CHUNK

    @pl.when(pl.program_id(1) == 0)
    def _():
        carry_ref[...] = jnp.zeros_like(carry_ref)

    x = x_ref[0]
    shift, scale, gate_mod = mod_ref[0, 3:4, :], mod_ref[0, 4:5, :], mod_ref[0, 5:6, :]
    h = _adaln(x, g_ref[2:3, :], shift, scale).astype(BF16)
    u_ref[...] = _dot(h, wup_ref[...])

    row = lax.broadcasted_iota(jnp.int32, (tm, FC), 0)

    def conv(cols):
        u = u_ref[:, cols]
        prev1 = carry_ref[V7X_SUBLANES - 1:V7X_SUBLANES, cols]
        prev2 = carry_ref[V7X_SUBLANES - 2:V7X_SUBLANES - 1, cols]
        u1 = jnp.where(row == 0, prev1, pltpu.roll(u, 1, 0))
        u2 = jnp.where(row == 0, prev2, jnp.where(row == 1, prev1, pltpu.roll(u, 2, 0)))
        return (cb_ref[0:1, cols] + cw_ref[0:1, cols] * u2 + cw_ref[1:2, cols] * u1
                + cw_ref[2:3, cols] * u)

    for c in range(F // FC):
        gate = conv(slice(c * FC, (c + 1) * FC))
        val = conv(slice(F + c * FC, F + (c + 1) * FC))
        act_ref[:, c * FC:(c + 1) * FC] = (gate * _sigmoid(gate) * val).astype(BF16)

    carry_ref[...] = u_ref[tm - V7X_SUBLANES:tm, :]
    y = _dot(act_ref[...], wdn_ref[...])
    o_ref[0] = x + (1.0 + gate_mod) * _rmsnorm(y, g_ref[3:4, :])


def _ffn_layer(x, mod, g, w_up, conv_w, conv_b, w_down, tm):
    B, S, Dm = x.shape
    F2 = w_up.shape[1]
    row_spec = pl.BlockSpec((1, tm, Dm), lambda b, i: (b, i, 0))
    return pl.pallas_call(
        functools.partial(_ffn_kernel, tm=tm),
        grid=(B, S // tm),
        in_specs=[
            row_spec,
            pl.BlockSpec((1, 6, Dm), lambda b, i: (b, 0, 0)),
            _resident((4, Dm)),
            _resident((Dm, F2)),
            _resident((CONV_WIDTH, F2)),
            _resident((1, F2)),
            _resident((F2 // 2, Dm)),
        ],
        out_specs=row_spec,
        out_shape=jax.ShapeDtypeStruct(x.shape, F32),
        scratch_shapes=[
            pltpu.VMEM((tm, F2), F32),
            pltpu.VMEM((tm, F2 // 2), BF16),
            pltpu.VMEM((V7X_SUBLANES, F2), F32),
        ],
        compiler_params=_params(),
        name="conv_ffn_sublayer",
    )(x, mod, g, w_up, conv_w, conv_b, w_down)


def _ret_kernel(x_ref, mod_ref, g_ref, win_ref, wout_ref, cos_ref, sin_ref, dmask_ref,
                xi_ref, zeta_ref, gch_ref, o_ref, proj_ref, go_ref, state_ref, *, tm):
    H, dk, dv, C = RET_HEADS, RET_QK_DIM, RET_V_DIM, RET_CHUNK
    half = dk // 2
    k_off, v_off, g_off = H * dk, 2 * H * dk, 2 * H * dk + H * dv

    @pl.when(pl.program_id(1) == 0)
    def _():
        state_ref[...] = jnp.zeros_like(state_ref)

    x = x_ref[0]
    shift, scale, gate_mod = mod_ref[0, 0:1, :], mod_ref[0, 1:2, :], mod_ref[0, 2:3, :]
    h = _adaln(x, g_ref[0:1, :], shift, scale).astype(BF16)
    proj_ref[...] = _dot(h, win_ref[...])

    def rotary(t, cos, sin):
        t1, t2 = t[:, :half], t[:, half:]
        return jnp.concatenate([t1 * cos - t2 * sin, t2 * cos + t1 * sin], axis=-1)

    for c in range(tm // C):
        rows = slice(c * C, (c + 1) * C)
        cos, sin = cos_ref[rows, :], sin_ref[rows, :]
        for hd in range(H):
            q = rotary(proj_ref[rows, hd * dk:(hd + 1) * dk], cos, sin).astype(BF16)
            k = (rotary(proj_ref[rows, k_off + hd * dk:k_off + (hd + 1) * dk], cos, sin)
                 * (dk ** -0.5)).astype(BF16)
            v = proj_ref[rows, v_off + hd * dv:v_off + (hd + 1) * dv]
            gate = proj_ref[rows, g_off + hd * dv:g_off + (hd + 1) * dv]
            s = lax.dot_general(q, k, (((1,), (1,)), ((), ())), preferred_element_type=F32)
            s = (s * dmask_ref[hd]).astype(BF16)
            inner = _dot(s, v.astype(BF16))
            state = state_ref[hd]
            cross = _dot(q, state.astype(BF16)) * xi_ref[hd]
            kv = lax.dot_general(k, (v * zeta_ref[hd]).astype(BF16), (((0,), (0,)), ((), ())),
                                 preferred_element_type=F32)
            state_ref[hd] = state * gch_ref[hd] + kv
            o = inner + cross
            mu = jnp.mean(o, axis=-1, keepdims=True)
            oc = o - mu
            var = jnp.mean(oc * oc, axis=-1, keepdims=True)
            on = oc * lax.rsqrt(var + NORM_EPS)
            go_ref[rows, hd * dv:(hd + 1) * dv] = (gate * _sigmoid(gate) * on).astype(BF16)

    y = _dot(go_ref[...], wout_ref[...])
    o_ref[0] = x + (1.0 + gate_mod) * _rmsnorm(y, g_ref[1:2, :])


def _retention_tables():
    H, C, dv = RET_HEADS, RET_CHUNK, RET_V_DIM
    log_gamma = jnp.log(1.0 - 2.0 ** (-5.0 - jnp.arange(H, dtype=F32)))
    idx = jnp.arange(C, dtype=F32)
    diff = idx[:, None] - idx[None, :]
    dmask = jnp.where(diff[None] >= 0,
                      jnp.exp(jnp.maximum(diff, 0.0)[None] * log_gamma[:, None, None]), 0.0)
    xi = jnp.exp((idx[None, :] + 1.0) * log_gamma[:, None])
    zeta = jnp.exp((C - 1.0 - idx[None, :]) * log_gamma[:, None])
    g_chunk = jnp.exp(C * log_gamma)
    xi_b = jnp.broadcast_to(xi[:, :, None], (H, C, dv))
    zeta_b = jnp.broadcast_to(zeta[:, :, None], (H, C, dv))
    return dmask, xi_b, zeta_b, g_chunk


def _rotary_tables(S):
    half = RET_QK_DIM // 2
    inv = ROPE_BASE ** (-jnp.arange(half, dtype=F32) / half)
    ang = jnp.arange(S).astype(F32)[:, None] * inv[None, :]
    return jnp.cos(ang), jnp.sin(ang)


def _retention_layer(x, mod, g, w_in, w_out, cos, sin, tables, tm):
    B, S, Dm = x.shape
    H, dk, dv, C = RET_HEADS, RET_QK_DIM, RET_V_DIM, RET_CHUNK
    dmask, xi_b, zeta_b, g_chunk = tables
    row_spec = pl.BlockSpec((1, tm, Dm), lambda b, i: (b, i, 0))
    pos_spec = pl.BlockSpec((tm, dk // 2), lambda b, i: (i, 0))
    return pl.pallas_call(
        functools.partial(_ret_kernel, tm=tm),
        grid=(B, S // tm),
        in_specs=[
            row_spec,
            pl.BlockSpec((1, 6, Dm), lambda b, i: (b, 0, 0)),
            _resident((4, Dm)),
            _resident(w_in.shape),
            _resident(w_out.shape),
            pos_spec,
            pos_spec,
            _resident((H, C, C)),
            _resident((H, C, dv)),
            _resident((H, C, dv)),
            pl.BlockSpec(memory_space=pltpu.SMEM),
        ],
        out_specs=row_spec,
        out_shape=jax.ShapeDtypeStruct(x.shape, F32),
        scratch_shapes=[
            pltpu.VMEM((tm, w_in.shape[1]), F32),
            pltpu.VMEM((tm, H * dv), BF16),
            pltpu.VMEM((H, dk, dv), F32),
        ],
        compiler_params=_params(),
        name="retention_sublayer",
    )(x, mod, g, w_in, w_out, cos, sin, dmask, xi_b, zeta_b, g_chunk)


def _kv_kernel(x_ref, mod_ref, g_ref, w_ref, k_ref, v_ref):
    n = k_ref.shape[-1]
    h = _adaln(x_ref[0], g_ref[...], mod_ref[0, 0:1, :], mod_ref[0, 1:2, :]).astype(BF16)
    kv = _dot(h, w_ref[...])
    k_ref[0] = kv[:, :n].astype(BF16)
    v_ref[0] = kv[:, n:].astype(BF16)


def _shared_kv(x, kv_mod, g, w_rep, tm):
    B, S, Dm = x.shape
    n = w_rep.shape[1] // 2
    row_spec = pl.BlockSpec((1, tm, Dm), lambda b, i: (b, i, 0))
    out_spec = pl.BlockSpec((1, tm, n), lambda b, i: (b, i, 0))
    return pl.pallas_call(
        _kv_kernel,
        grid=(B, S // tm),
        in_specs=[
            row_spec,
            pl.BlockSpec((1, 2, Dm), lambda b, i: (b, 0, 0)),
            _resident((1, Dm)),
            _resident(w_rep.shape),
        ],
        out_specs=[out_spec, out_spec],
        out_shape=[jax.ShapeDtypeStruct((B, S, n), BF16)] * 2,
        compiler_params=_params(),
        name="shared_kv",
    )(x, kv_mod, g, w_rep)


def _bias_kernel(table_ref, bucket_ref, win_ref, o_ref):
    hd = pl.program_id(0)
    bucket = bucket_ref[...]
    acc = jnp.zeros(bucket.shape, F32)
    for b in range(REL_BUCKETS):
        acc = jnp.where(bucket == b, table_ref[b, hd], acc)
    o_ref[0] = jnp.where(win_ref[...] != 0, acc, MASK_NEG)


def _bias_band(rel_bias):
    i = np.arange(BLOCK)[:, None]
    j = np.arange(2 * BLOCK)[None, :]
    dist = i + BLOCK - j
    n = np.maximum(dist, 0)
    max_exact = REL_BUCKETS // 2
    large = max_exact + (np.log(np.maximum(n, 1).astype(np.float32) / max_exact)
                         / math.log(REL_MAX_DIST / max_exact)
                         * (REL_BUCKETS - max_exact)).astype(np.int32)
    large = np.minimum(large, REL_BUCKETS - 1)
    bucket = np.where(n < max_exact, n, large).astype(np.int32)
    win = ((dist >= 0) & (dist < WINDOW)).astype(np.int32)
    full = pl.BlockSpec((BLOCK, 2 * BLOCK), lambda h: (0, 0))
    return pl.pallas_call(
        _bias_kernel,
        grid=(SWA_HEADS,),
        in_specs=[pl.BlockSpec(memory_space=pltpu.SMEM), full, full],
        out_specs=pl.BlockSpec((1, BLOCK, 2 * BLOCK), lambda h: (h, 0, 0)),
        out_shape=jax.ShapeDtypeStruct((SWA_HEADS, BLOCK, 2 * BLOCK), F32),
        name="rel_bias_band",
    )(rel_bias, jnp.asarray(bucket), jnp.asarray(win))


def _swa_kernel(x_ref, mod_ref, g_ref, wq_ref, wo_ref, kc_ref, kp_ref, vc_ref, vp_ref,
                bias_ref, sink_ref, o_ref, q_ref, att_ref, *, tm):
    KV, GL, GQ = SWA_KV_HEADS, GQ_LANES, GQ_ROWS
    first_tile = pl.program_id(1) == 0

    x = x_ref[0]
    shift, scale, gate_mod = mod_ref[0, 0:1, :], mod_ref[0, 1:2, :], mod_ref[0, 2:3, :]
    h = _adaln(x, g_ref[0:1, :], shift, scale).astype(BF16)
    q_ref[...] = _dot(h, wq_ref[...]) * (SWA_HEAD_DIM ** -0.5)

    lane_group = lax.broadcasted_iota(jnp.int32, (BLOCK, GL), 1) // SWA_HEAD_DIM
    key_lane = lax.broadcasted_iota(jnp.int32, (GQ, 2 * BLOCK), 1)
    edge_mask = jnp.where(jnp.logical_and(first_tile, key_lane < BLOCK), MASK_NEG, 0.0)

    for j in range(tm // BLOCK):
        rows = slice(j * BLOCK, (j + 1) * BLOCK)
        prev = slice((j - 1) * BLOCK, j * BLOCK)
        for kh in range(KV):
            lanes = slice(kh * GL, (kh + 1) * GL)
            k_prev = kp_ref[0, :, lanes] if j == 0 else kc_ref[0, prev, lanes]
            v_prev = vp_ref[0, :, lanes] if j == 0 else vc_ref[0, prev, lanes]
            keys = jnp.concatenate([k_prev, kc_ref[0, rows, lanes]], axis=0)
            vals = jnp.concatenate([v_prev, vc_ref[0, rows, lanes]], axis=0)
            q = q_ref[rows, lanes]
            qg = jnp.concatenate(
                [jnp.where(lane_group == g, q, 0.0) for g in range(SWA_GROUP)], axis=0).astype(BF16)
            s = lax.dot_general(qg, keys, (((1,), (1,)), ((), ())), preferred_element_type=F32)
            s = s + bias_ref[kh]
            if j == 0:
                s = s + edge_mask
            sink = sink_ref[kh]
            m = jnp.maximum(jnp.max(s, axis=-1, keepdims=True), sink)
            e = jnp.exp(s - m)
            p = e / (jnp.sum(e, axis=-1, keepdims=True) + jnp.exp(sink - m))
            o = _dot(p.astype(BF16), vals)
            att = jnp.zeros((BLOCK, GL), F32)
            for g in range(SWA_GROUP):
                att = att + jnp.where(lane_group == g, o[g * BLOCK:(g + 1) * BLOCK], 0.0)
            att_ref[rows, lanes] = att.astype(BF16)

    y = _dot(att_ref[...], wo_ref[...])
    o_ref[0] = x + (1.0 + gate_mod) * _rmsnorm(y, g_ref[1:2, :])


def _swa_layer(x, mod, g, w_q, w_o, k_rep, v_rep, bias, sinks, tm):
    B, S, Dm = x.shape
    n = k_rep.shape[-1]
    nb = tm // BLOCK
    row_spec = pl.BlockSpec((1, tm, Dm), lambda b, i: (b, i, 0))
    cur_spec = pl.BlockSpec((1, tm, n), lambda b, i: (b, i, 0))
    prev_spec = pl.BlockSpec((1, BLOCK, n), lambda b, i: (b, jnp.maximum(i * nb - 1, 0), 0))
    return pl.pallas_call(
        functools.partial(_swa_kernel, tm=tm),
        grid=(B, S // tm),
        in_specs=[
            row_spec,
            pl.BlockSpec((1, 6, Dm), lambda b, i: (b, 0, 0)),
            _resident((4, Dm)),
            _resident(w_q.shape),
            _resident(w_o.shape),
            cur_spec, prev_spec, cur_spec, prev_spec,
            _resident(bias.shape),
            _resident(sinks.shape),
        ],
        out_specs=row_spec,
        out_shape=jax.ShapeDtypeStruct(x.shape, F32),
        scratch_shapes=[
            pltpu.VMEM((tm, w_q.shape[1]), F32),
            pltpu.VMEM((tm, w_o.shape[0]), BF16),
        ],
        compiler_params=_params(),
        name="swa_sublayer",
    )(x, mod, g, w_q, w_o, k_rep, k_rep, v_rep, v_rep, bias, sinks)


def _repeat_kv_weight(kv_w):
    Dm = kv_w.shape[0]
    w = kv_w.reshape(Dm, 2, SWA_KV_HEADS, 1, SWA_HEAD_DIM)
    w = jnp.broadcast_to(w, (Dm, 2, SWA_KV_HEADS, SWA_GROUP, SWA_HEAD_DIM))
    return w.reshape(Dm, 2 * SWA_HEADS * SWA_HEAD_DIM)


def kernel(x, c, norm_g, ada_w, ada_b, ret_w_in, ret_w_out, kv_norm_g, kv_ada_w, kv_ada_b, kv_w,
           swa_w_q, swa_w_o, swa_sinks, rel_bias, ffn_w_up, ffn_conv_w, ffn_conv_b, ffn_w_down):
    B, S, Dm = x.shape
    TM_FFN, TM_RET, TM_SWA, TM_KV = 256, 256, 256, 512

    c_pad = jnp.zeros((V7X_SUBLANES, Dm), F32).at[:B].set(c)
    mod = _modulation(c_pad, ada_w, ada_b[:, None, :], 2048)
    mod = mod[:, :B].reshape(DEPTH, B, 6, Dm)
    kv_mod = _modulation(c_pad, kv_ada_w[None], kv_ada_b[None, None, :], 2048)
    kv_mod = kv_mod[0, :B].reshape(B, 2, Dm)

    cos, sin = _rotary_tables(S)
    tables = _retention_tables()
    bias = _bias_band(rel_bias).reshape(SWA_KV_HEADS, GQ_ROWS, 2 * BLOCK)
    sink_cols = jnp.repeat(swa_sinks.astype(F32), BLOCK, axis=1).reshape(
        swa_sinks.shape[0], SWA_KV_HEADS, GQ_ROWS, 1)

    k_rep = v_rep = None
    for l in range(DEPTH):
        if l < N_A_LAYERS:
            x = _retention_layer(x, mod[l], norm_g[l], ret_w_in[l].astype(BF16),
                                 ret_w_out[l].astype(BF16), cos, sin, tables, TM_RET)
        else:
            i = l - N_A_LAYERS
            x = _swa_layer(x, mod[l], norm_g[l], swa_w_q[i].astype(BF16), swa_w_o[i].astype(BF16),
                           k_rep, v_rep, bias, sink_cols[i], TM_SWA)
        x = _ffn_layer(x, mod[l], norm_g[l], ffn_w_up[l].astype(BF16), ffn_conv_w[l],
                       ffn_conv_b[l][None, :], ffn_w_down[l].astype(BF16), TM_FFN)
        if l == N_A_LAYERS - 1:
            k_rep, v_rep = _shared_kv(x, kv_mod, kv_norm_g[None, :],
                                      _repeat_kv_weight(kv_w).astype(BF16), TM_KV)
    return x
```

```python
import functools
import math

import jax
import jax.numpy as jnp
import numpy as np
from jax import lax
from jax.experimental import pallas as pl
from jax.experimental.pallas import tpu as pltpu

D_MODEL = 1024
DEPTH = 4
N_A_LAYERS = DEPTH // 2
RET_HEADS = 4
RET_QK_DIM = D_MODEL // RET_HEADS
RET_V_DIM = 2 * D_MODEL // RET_HEADS
RET_CHUNK = 128
ROPE_BASE = 10000.0
SWA_HEADS = 16
SWA_KV_HEADS = 4
SWA_GROUP = SWA_HEADS // SWA_KV_HEADS
SWA_HEAD_DIM = 64
WINDOW = 128
BLOCK = WINDOW
REL_BUCKETS = 32
REL_MAX_DIST = 128
D_FF = 2816
CONV_WIDTH = 3
NORM_EPS = 1e-6

V7X_SUBLANES = 8
V7X_LANES = 128
V7X_MXU_DIM = 256
V7X_VMEM_LIMIT_BYTES = 58 * 1024 * 1024

MASK_NEG = -1e30
BF16 = jnp.bfloat16
F32 = jnp.float32

FFN_COL_CHUNK = V7X_MXU_DIM
GQ_ROWS = SWA_GROUP * BLOCK
K_PAD_LANES = V7X_LANES


def _dot(a, b):
    return jnp.dot(a, b, preferred_element_type=F32)


def _dot_nt(a, b):
    return lax.dot_general(a, b, (((1,), (1,)), ((), ())), preferred_element_type=F32)


def _dot_tn(a, b):
    return lax.dot_general(a, b, (((0,), (0,)), ((), ())), preferred_element_type=F32)


def _sigmoid(x):
    return 1.0 / (1.0 + jnp.exp(-x))


def _adaln(x, g, shift, scale):
    ms = jnp.mean(x * x, axis=-1, keepdims=True)
    y = x * lax.rsqrt(ms + NORM_EPS) * g
    return y * (1.0 + scale) + shift


def _rmsnorm(y, g):
    ms = jnp.mean(y * y, axis=-1, keepdims=True)
    return y * lax.rsqrt(ms + NORM_EPS) * g


def _resident(shape):
    zeros = (0,) * len(shape)
    return pl.BlockSpec(shape, lambda b, i: zeros, pipeline_mode=pl.Buffered(1))


def _params():
    return pltpu.CompilerParams(
        dimension_semantics=("arbitrary", "arbitrary"),
        vmem_limit_bytes=V7X_VMEM_LIMIT_BYTES)


def _mod_kernel(c_ref, w_ref, b_ref, o_ref):
    c = c_ref[...]
    ca = (c * _sigmoid(c)).astype(BF16)
    o_ref[0] = _dot(ca, w_ref[0].astype(BF16)) + b_ref[0]


def _modulation(c_pad, w, b, tn):
    L, Dm, N = w.shape
    return pl.pallas_call(
        _mod_kernel,
        grid=(L, N // tn),
        in_specs=[
            pl.BlockSpec((V7X_SUBLANES, Dm), lambda l, j: (0, 0)),
            pl.BlockSpec((1, Dm, tn), lambda l, j: (l, 0, j)),
            pl.BlockSpec((1, 1, tn), lambda l, j: (l, 0, j)),
        ],
        out_specs=pl.BlockSpec((1, V7X_SUBLANES, tn), lambda l, j: (l, 0, j)),
        out_shape=jax.ShapeDtypeStruct((L, V7X_SUBLANES, N), F32),
        compiler_params=_params(),
        name="adaln_modulation",
    )(c_pad, w, b)


def _ffn_kernel(x_ref, mod_ref, g_ref, wup_ref, cw_ref, cb_ref, wdn_ref, o_ref,
                u_ref, act_ref, carry_ref, *, tm):
    F = D_FF
    FC = FFN_COL_CHUNK

    @pl.when(pl.program_id(1) == 0)
    def _():
        carry_ref[...] = jnp.zeros_like(carry_ref)

    x = x_ref[0]
    shift, scale, gate_mod = mod_ref[0, 3:4, :], mod_ref[0, 4:5, :], mod_ref[0, 5:6, :]
    h = _adaln(x, g_ref[2:3, :], shift, scale).astype(BF16)
    u_ref[...] = _dot(h, wup_ref[...])

    row = lax.broadcasted_iota(jnp.int32, (tm, FC), 0)

    def conv(cols):
        u = u_ref[:, cols]
        prev1 = carry_ref[V7X_SUBLANES - 1:V7X_SUBLANES, cols]
        prev2 = carry_ref[V7X_SUBLANES - 2:V7X_SUBLANES - 1, cols]
        u1 = jnp.where(row == 0, prev1, pltpu.roll(u, 1, 0))
        u2 = jnp.where(row == 0, prev2, jnp.where(row == 1, prev1, pltpu.roll(u, 2, 0)))
        return (cb_ref[0:1, cols] + cw_ref[0:1, cols] * u2 + cw_ref[1:2, cols] * u1
                + cw_ref[2:3, cols] * u)

    for c in range(F // FC):
        gate = conv(slice(c * FC, (c + 1) * FC))
        val = conv(slice(F + c * FC, F + (c + 1) * FC))
        act_ref[:, c * FC:(c + 1) * FC] = (gate * _sigmoid(gate) * val).astype(BF16)

    carry_ref[...] = u_ref[tm - V7X_SUBLANES:tm, :]
    y = _dot(act_ref[...], wdn_ref[...])
    o_ref[0] = x + (1.0 + gate_mod) * _rmsnorm(y, g_ref[3:4, :])


def _ffn_layer(x, mod, g, w_up, conv_w, conv_b, w_down, tm):
    B, S, Dm = x.shape
    F2 = w_up.shape[1]
    row_spec = pl.BlockSpec((1, tm, Dm), lambda b, i: (b, i, 0))
    return pl.pallas_call(
        functools.partial(_ffn_kernel, tm=tm),
        grid=(B, S // tm),
        in_specs=[
            row_spec,
            pl.BlockSpec((1, 6, Dm), lambda b, i: (b, 0, 0)),
            _resident((4, Dm)),
            _resident((Dm, F2)),
            _resident((CONV_WIDTH, F2)),
            _resident((1, F2)),
            _resident((F2 // 2, Dm)),
        ],
        out_specs=row_spec,
        out_shape=jax.ShapeDtypeStruct(x.shape, F32),
        scratch_shapes=[
            pltpu.VMEM((tm, F2), F32),
            pltpu.VMEM((tm, F2 // 2), BF16),
            pltpu.VMEM((V7X_SUBLANES, F2), F32),
        ],
        compiler_params=_params(),
        name="conv_ffn_sublayer",
    )(x, mod, g, w_up, conv_w, conv_b, w_down)


def _ret_kernel(x_ref, mod_ref, g_ref, win_ref, wout_ref, cos_ref, sin_ref, dmask_ref,
                xi_ref, zeta_ref, gch_ref, o_ref, proj_ref, go_ref, state_ref, *, tm):
    H, dk, dv, C = RET_HEADS, RET_QK_DIM, RET_V_DIM, RET_CHUNK
    half = dk // 2
    k_off, v_off, g_off = H * dk, 2 * H * dk, 2 * H * dk + H * dv

    @pl.when(pl.program_id(1) == 0)
    def _():
        state_ref[...] = jnp.zeros_like(state_ref)

    x = x_ref[0]
    shift, scale, gate_mod = mod_ref[0, 0:1, :], mod_ref[0, 1:2, :], mod_ref[0, 2:3, :]
    h = _adaln(x, g_ref[0:1, :], shift, scale).astype(BF16)
    proj_ref[...] = _dot(h, win_ref[...])

    def rotary(t, cos, sin):
        t1, t2 = t[:, :half], t[:, half:]
        return jnp.concatenate([t1 * cos - t2 * sin, t2 * cos + t1 * sin], axis=-1)

    for c in range(tm // C):
        rows = slice(c * C, (c + 1) * C)
        cos, sin = cos_ref[rows, :], sin_ref[rows, :]
        for hd in range(H):
            q = rotary(proj_ref[rows, hd * dk:(hd + 1) * dk], cos, sin).astype(BF16)
            k = (rotary(proj_ref[rows, k_off + hd * dk:k_off + (hd + 1) * dk], cos, sin)
                 * (dk ** -0.5)).astype(BF16)
            v = proj_ref[rows, v_off + hd * dv:v_off + (hd + 1) * dv]
            gate = proj_ref[rows, g_off + hd * dv:g_off + (hd + 1) * dv]
            s = _dot_nt(q, k)
            s = (s * dmask_ref[hd]).astype(BF16)
            inner = _dot(s, v.astype(BF16))
            state = state_ref[hd]
            cross = _dot(q, state.astype(BF16)) * xi_ref[hd]
            kv = _dot_tn(k, (v * zeta_ref[hd]).astype(BF16))
            state_ref[hd] = state * gch_ref[hd] + kv
            o = inner + cross
            mu = jnp.mean(o, axis=-1, keepdims=True)
            oc = o - mu
            var = jnp.mean(oc * oc, axis=-1, keepdims=True)
            on = oc * lax.rsqrt(var + NORM_EPS)
            go_ref[rows, hd * dv:(hd + 1) * dv] = (gate * _sigmoid(gate) * on).astype(BF16)

    y = _dot(go_ref[...], wout_ref[...])
    o_ref[0] = x + (1.0 + gate_mod) * _rmsnorm(y, g_ref[1:2, :])


def _retention_tables():
    H, C, dv = RET_HEADS, RET_CHUNK, RET_V_DIM
    log_gamma = jnp.log(1.0 - 2.0 ** (-5.0 - jnp.arange(H, dtype=F32)))
    idx = jnp.arange(C, dtype=F32)
    diff = idx[:, None] - idx[None, :]
    dmask = jnp.where(diff[None] >= 0,
                      jnp.exp(jnp.maximum(diff, 0.0)[None] * log_gamma[:, None, None]), 0.0)
    xi = jnp.exp((idx[None, :] + 1.0) * log_gamma[:, None])
    zeta = jnp.exp((C - 1.0 - idx[None, :]) * log_gamma[:, None])
    g_chunk = jnp.exp(C * log_gamma)
    xi_b = jnp.broadcast_to(xi[:, :, None], (H, C, dv))
    zeta_b = jnp.broadcast_to(zeta[:, :, None], (H, C, dv))
    return dmask, xi_b, zeta_b, g_chunk


def _rotary_tables(S):
    half = RET_QK_DIM // 2
    inv = ROPE_BASE ** (-jnp.arange(half, dtype=F32) / half)
    ang = jnp.arange(S).astype(F32)[:, None] * inv[None, :]
    return jnp.cos(ang), jnp.sin(ang)


def _retention_layer(x, mod, g, w_in, w_out, cos, sin, tables, tm):
    B, S, Dm = x.shape
    H, dk, dv, C = RET_HEADS, RET_QK_DIM, RET_V_DIM, RET_CHUNK
    dmask, xi_b, zeta_b, g_chunk = tables
    row_spec = pl.BlockSpec((1, tm, Dm), lambda b, i: (b, i, 0))
    pos_spec = pl.BlockSpec((tm, dk // 2), lambda b, i: (i, 0))
    return pl.pallas_call(
        functools.partial(_ret_kernel, tm=tm),
        grid=(B, S // tm),
        in_specs=[
            row_spec,
            pl.BlockSpec((1, 6, Dm), lambda b, i: (b, 0, 0)),
            _resident((4, Dm)),
            _resident(w_in.shape),
            _resident(w_out.shape),
            pos_spec,
            pos_spec,
            _resident((H, C, C)),
            _resident((H, C, dv)),
            _resident((H, C, dv)),
            pl.BlockSpec(memory_space=pltpu.SMEM),
        ],
        out_specs=row_spec,
        out_shape=jax.ShapeDtypeStruct(x.shape, F32),
        scratch_shapes=[
            pltpu.VMEM((tm, w_in.shape[1]), F32),
            pltpu.VMEM((tm, H * dv), BF16),
            pltpu.VMEM((H, dk, dv), F32),
        ],
        compiler_params=_params(),
        name="retention_sublayer",
    )(x, mod, g, w_in, w_out, cos, sin, dmask, xi_b, zeta_b, g_chunk)


def _kv_kernel(x_ref, mod_ref, g_ref, wk_ref, wvt_ref, k_ref, vt_ref):
    h = _adaln(x_ref[0], g_ref[...], mod_ref[0, 0:1, :], mod_ref[0, 1:2, :]).astype(BF16)
    k_ref[0] = _dot(h, wk_ref[...]).astype(BF16)
    vt_ref[0] = _dot_nt(wvt_ref[...], h).astype(BF16)


def _shared_kv(x, kv_mod, g, wk_pad, wv_t, tm):
    B, S, Dm = x.shape
    nk, nv = wk_pad.shape[1], wv_t.shape[0]
    return pl.pallas_call(
        _kv_kernel,
        grid=(B, S // tm),
        in_specs=[
            pl.BlockSpec((1, tm, Dm), lambda b, i: (b, i, 0)),
            pl.BlockSpec((1, 2, Dm), lambda b, i: (b, 0, 0)),
            _resident((1, Dm)),
            _resident(wk_pad.shape),
            _resident(wv_t.shape),
        ],
        out_specs=[pl.BlockSpec((1, tm, nk), lambda b, i: (b, i, 0)),
                   pl.BlockSpec((1, nv, tm), lambda b, i: (b, 0, i))],
        out_shape=[jax.ShapeDtypeStruct((B, S, nk), BF16),
                   jax.ShapeDtypeStruct((B, nv, S), BF16)],
        compiler_params=_params(),
        name="shared_kv",
    )(x, kv_mod, g, wk_pad, wv_t)


def _bias_kernel(table_ref, bucket_ref, win_ref, o_ref):
    hd = pl.program_id(0)
    bucket = bucket_ref[...]
    acc = jnp.zeros(bucket.shape, F32)
    for b in range(REL_BUCKETS):
        acc = jnp.where(bucket == b, table_ref[b, hd], acc)
    o_ref[0] = jnp.where(win_ref[...] != 0, acc, MASK_NEG)


def _bias_band_t(rel_bias):
    i = np.arange(BLOCK)[None, :]
    j = np.arange(2 * BLOCK)[:, None]
    dist = i + BLOCK - j
    n = np.maximum(dist, 0)
    max_exact = REL_BUCKETS // 2
    large = max_exact + (np.log(np.maximum(n, 1).astype(np.float32) / max_exact)
                         / math.log(REL_MAX_DIST / max_exact)
                         * (REL_BUCKETS - max_exact)).astype(np.int32)
    large = np.minimum(large, REL_BUCKETS - 1)
    bucket = np.where(n < max_exact, n, large).astype(np.int32)
    win = ((dist >= 0) & (dist < WINDOW)).astype(np.int32)
    full = pl.BlockSpec((2 * BLOCK, BLOCK), lambda h: (0, 0))
    return pl.pallas_call(
        _bias_kernel,
        grid=(SWA_HEADS,),
        in_specs=[pl.BlockSpec(memory_space=pltpu.SMEM), full, full],
        out_specs=pl.BlockSpec((1, 2 * BLOCK, BLOCK), lambda h: (h, 0, 0)),
        out_shape=jax.ShapeDtypeStruct((SWA_HEADS, 2 * BLOCK, BLOCK), F32),
        name="rel_bias_band",
    )(rel_bias, jnp.asarray(bucket), jnp.asarray(win))


def _swa_kernel(x_ref, mod_ref, g_ref, wqt_ref, wo_ref, kc_ref, kp_ref, vtc_ref, vtp_ref,
                bias_ref, sink_ref, o_ref, qt_ref, att_ref, *, tm):
    KV, G, hd, KL = SWA_KV_HEADS, SWA_GROUP, SWA_HEAD_DIM, K_PAD_LANES
    n_q = SWA_HEADS * hd
    first_tile = pl.program_id(1) == 0

    x = x_ref[0]
    shift, scale, gate_mod = mod_ref[0, 0:1, :], mod_ref[0, 1:2, :], mod_ref[0, 2:3, :]
    h = _adaln(x, g_ref[0:1, :], shift, scale).astype(BF16)
    qt_ref[0:n_q, :] = (_dot_nt(wqt_ref[...], h) * (hd ** -0.5)).astype(BF16)
    qt_ref[n_q:n_q + hd, :] = jnp.zeros((hd, tm), BF16)

    key_row = lax.broadcasted_iota(jnp.int32, (2 * BLOCK, GQ_ROWS), 0)
    edge_mask = jnp.where(jnp.logical_and(first_tile, key_row < BLOCK), MASK_NEG, 0.0)

    for j in range(tm // BLOCK):
        cols = slice(j * BLOCK, (j + 1) * BLOCK)
        band = slice((j - 1) * BLOCK, (j + 1) * BLOCK)
        for kh in range(KV):
            klanes = slice(kh * KL, (kh + 1) * KL)
            vrows = slice(kh * hd, (kh + 1) * hd)
            if j == 0:
                keys = jnp.concatenate([kp_ref[0, :, klanes], kc_ref[0, cols, klanes]], axis=0)
                vals_t = jnp.concatenate([vtp_ref[0, vrows, :], vtc_ref[0, vrows, cols]], axis=1)
            else:
                keys = kc_ref[0, band, klanes]
                vals_t = vtc_ref[0, vrows, band]
            q_cat = jnp.concatenate(
                [qt_ref[(kh * G + g) * hd:(kh * G + g) * hd + KL, cols] for g in range(G)],
                axis=1)
            s = _dot(keys, q_cat) + bias_ref[kh]
            if j == 0:
                s = s + edge_mask
            sink = sink_ref[kh]
            m = jnp.maximum(jnp.max(s, axis=0, keepdims=True), sink)
            e = jnp.exp(s - m)
            inv = 1.0 / (jnp.sum(e, axis=0, keepdims=True) + jnp.exp(sink - m))
            o_t = _dot(vals_t, (e * inv).astype(BF16))
            for g in range(G):
                att_ref[(kh * G + g) * hd:(kh * G + g + 1) * hd, cols] = (
                    o_t[:, g * BLOCK:(g + 1) * BLOCK].astype(BF16))

    y = _dot_tn(att_ref[...], wo_ref[...])
    o_ref[0] = x + (1.0 + gate_mod) * _rmsnorm(y, g_ref[1:2, :])


def _swa_layer(x, mod, g, w_q_t, w_o, k_pad, v_t, bias_t, sink_rows, tm):
    B, S, Dm = x.shape
    nk, nv = k_pad.shape[-1], v_t.shape[1]
    n_q = w_q_t.shape[0]
    nb = tm // BLOCK
    row_spec = pl.BlockSpec((1, tm, Dm), lambda b, i: (b, i, 0))
    prev_blk = lambda i: jnp.maximum(i * nb - 1, 0)
    return pl.pallas_call(
        functools.partial(_swa_kernel, tm=tm),
        grid=(B, S // tm),
        in_specs=[
            row_spec,
            pl.BlockSpec((1, 6, Dm), lambda b, i: (b, 0, 0)),
            _resident((4, Dm)),
            _resident(w_q_t.shape),
            _resident(w_o.shape),
            pl.BlockSpec((1, tm, nk), lambda b, i: (b, i, 0)),
            pl.BlockSpec((1, BLOCK, nk), lambda b, i: (b, prev_blk(i), 0)),
            pl.BlockSpec((1, nv, tm), lambda b, i: (b, 0, i)),
            pl.BlockSpec((1, nv, BLOCK), lambda b, i: (b, 0, prev_blk(i))),
            _resident(bias_t.shape),
            _resident(sink_rows.shape),
        ],
        out_specs=row_spec,
        out_shape=jax.ShapeDtypeStruct(x.shape, F32),
        scratch_shapes=[
            pltpu.VMEM((n_q + SWA_HEAD_DIM, tm), BF16),
            pltpu.VMEM((n_q, tm), BF16),
        ],
        compiler_params=_params(),
        name="swa_sublayer",
    )(x, mod, g, w_q_t, w_o, k_pad, k_pad, v_t, v_t, bias_t, sink_rows)


def _kv_weights(kv_w):
    Dm = kv_w.shape[0]
    n = SWA_KV_HEADS * SWA_HEAD_DIM
    wk = kv_w[:, :n].reshape(Dm, SWA_KV_HEADS, SWA_HEAD_DIM)
    wk = jnp.pad(wk, ((0, 0), (0, 0), (0, K_PAD_LANES - SWA_HEAD_DIM)))
    return wk.reshape(Dm, SWA_KV_HEADS * K_PAD_LANES).astype(BF16), kv_w[:, n:].T.astype(BF16)


def kernel(x, c, norm_g, ada_w, ada_b, ret_w_in, ret_w_out, kv_norm_g, kv_ada_w, kv_ada_b, kv_w,
           swa_w_q, swa_w_o, swa_sinks, rel_bias, ffn_w_up, ffn_conv_w, ffn_conv_b, ffn_w_down):
    B, S, Dm = x.shape
    TM_FFN, TM_RET, TM_SWA, TM_KV = 256, 256, 256, 512

    c_pad = jnp.zeros((V7X_SUBLANES, Dm), F32).at[:B].set(c)
    mod = _modulation(c_pad, ada_w, ada_b[:, None, :], 2048)
    mod = mod[:, :B].reshape(DEPTH, B, 6, Dm)
    kv_mod = _modulation(c_pad, kv_ada_w[None], kv_ada_b[None, None, :], 2048)
    kv_mod = kv_mod[0, :B].reshape(B, 2, Dm)

    cos, sin = _rotary_tables(S)
    tables = _retention_tables()
    bias_t = _bias_band_t(rel_bias).reshape(SWA_KV_HEADS, SWA_GROUP, 2 * BLOCK, BLOCK)
    bias_t = bias_t.transpose(0, 2, 1, 3).reshape(SWA_KV_HEADS, 2 * BLOCK, GQ_ROWS)
    sink_rows = jnp.repeat(swa_sinks.astype(F32), BLOCK, axis=1).reshape(
        swa_sinks.shape[0], SWA_KV_HEADS, 1, GQ_ROWS)

    k_pad = v_t = None
    for l in range(DEPTH):
        if l < N_A_LAYERS:
            x = _retention_layer(x, mod[l], norm_g[l], ret_w_in[l].astype(BF16),
                                 ret_w_out[l].astype(BF16), cos, sin, tables, TM_RET)
        else:
            i = l - N_A_LAYERS
            x = _swa_layer(x, mod[l], norm_g[l], swa_w_q[i].T.astype(BF16), swa_w_o[i].astype(BF16),
                           k_pad, v_t, bias_t, sink_rows[i], TM_SWA)
        x = _ffn_layer(x, mod[l], norm_g[l], ffn_w_up[l].astype(BF16), ffn_conv_w[l],
                       ffn_conv_b[l][None, :], ffn_w_down[l].astype(BF16), TM_FFN)
        if l == N_A_LAYERS - 1:
            wk_pad, wv_t = _kv_weights(kv_w)
            k_pad, v_t = _shared_kv(x, kv_mod, kv_norm_g[None, :], wk_pad, wv_t, TM_KV)
    return x
```

```python
import functools
import math

import jax
import jax.numpy as jnp
import numpy as np
from jax import lax
from jax.experimental import pallas as pl
from jax.experimental.pallas import tpu as pltpu

D_MODEL = 1024
DEPTH = 4
N_A_LAYERS = DEPTH // 2
RET_HEADS = 4
RET_QK_DIM = D_MODEL // RET_HEADS
RET_V_DIM = 2 * D_MODEL // RET_HEADS
RET_CHUNK = 128
ROPE_BASE = 10000.0
SWA_HEADS = 16
SWA_KV_HEADS = 4
SWA_GROUP = SWA_HEADS // SWA_KV_HEADS
SWA_HEAD_DIM = 64
WINDOW = 128
BLOCK = WINDOW
REL_BUCKETS = 32
REL_MAX_DIST = 128
D_FF = 2816
CONV_WIDTH = 3
NORM_EPS = 1e-6

V7X_SUBLANES = 8
V7X_LANES = 128
V7X_MXU_DIM = 256
V7X_VMEM_LIMIT_BYTES = 58 * 1024 * 1024

MASK_NEG = -1e30
BF16 = jnp.bfloat16
F32 = jnp.float32

RET_KERNEL_CHUNK = V7X_MXU_DIM
FFN_COL_CHUNK = V7X_MXU_DIM
GQ_ROWS = SWA_GROUP * BLOCK
K_PAD_LANES = V7X_LANES


def _dot(a, b):
    return jnp.dot(a, b, preferred_element_type=F32)


def _dot_nt(a, b):
    return lax.dot_general(a, b, (((1,), (1,)), ((), ())), preferred_element_type=F32)


def _dot_tn(a, b):
    return lax.dot_general(a, b, (((0,), (0,)), ((), ())), preferred_element_type=F32)


def _sigmoid(x):
    return 1.0 / (1.0 + jnp.exp(-x))


def _adaln(x, g, shift, scale):
    ms = jnp.mean(x * x, axis=-1, keepdims=True)
    y = x * lax.rsqrt(ms + NORM_EPS) * g
    return y * (1.0 + scale) + shift


def _rmsnorm(y, g):
    ms = jnp.mean(y * y, axis=-1, keepdims=True)
    return y * lax.rsqrt(ms + NORM_EPS) * g


def _resident(shape):
    zeros = (0,) * len(shape)
    return pl.BlockSpec(shape, lambda b, i: zeros, pipeline_mode=pl.Buffered(1))


def _resident_layer(stacked, l):
    tail = (0,) * (stacked.ndim - 1)
    return pl.BlockSpec((None,) + stacked.shape[1:], lambda b, i: (l,) + tail,
                        pipeline_mode=pl.Buffered(1))


def _mod_spec(mod, l):
    return pl.BlockSpec((None, 1) + mod.shape[2:], lambda b, i: (l, b, 0, 0))


def _params():
    return pltpu.CompilerParams(
        dimension_semantics=("arbitrary", "arbitrary"),
        vmem_limit_bytes=V7X_VMEM_LIMIT_BYTES)


def _mod_kernel(c_ref, w_ref, b_ref, o_ref):
    c = c_ref[...]
    ca = (c * _sigmoid(c)).astype(BF16)
    o_ref[0] = _dot(ca, w_ref[0].astype(BF16)) + b_ref[0]


def _modulation(c_pad, w, b, tn):
    L, Dm, N = w.shape
    return pl.pallas_call(
        _mod_kernel,
        grid=(L, N // tn),
        in_specs=[
            pl.BlockSpec((V7X_SUBLANES, Dm), lambda l, j: (0, 0)),
            pl.BlockSpec((1, Dm, tn), lambda l, j: (l, 0, j)),
            pl.BlockSpec((1, 1, tn), lambda l, j: (l, 0, j)),
        ],
        out_specs=pl.BlockSpec((1, V7X_SUBLANES, tn), lambda l, j: (l, 0, j)),
        out_shape=jax.ShapeDtypeStruct((L, V7X_SUBLANES, N), F32),
        compiler_params=_params(),
        name="adaln_modulation",
    )(c_pad, w, b)


def _ffn_kernel(x_ref, mod_ref, g_ref, wup_ref, cw_ref, cb_ref, wdn_ref, o_ref,
                act_ref, u_ref, *, tm, sub):
    F = D_FF
    FC = FFN_COL_CHUNK

    @pl.when(pl.program_id(1) == 0)
    def _():
        u_ref[sub:sub + V7X_SUBLANES, :] = jnp.zeros((V7X_SUBLANES, 2 * F), F32)

    shift, scale, gate_mod = mod_ref[0, 3:4, :], mod_ref[0, 4:5, :], mod_ref[0, 5:6, :]
    HALO = V7X_SUBLANES

    def conv(h, cols):
        u = _dot(h, wup_ref[:, cols])
        u_ref[0:HALO, cols] = u_ref[sub:sub + HALO, cols]
        u_ref[HALO:HALO + sub, cols] = u
        u1 = u_ref[HALO - 1:HALO - 1 + sub, cols]
        u2 = u_ref[HALO - 2:HALO - 2 + sub, cols]
        return (cb_ref[0:1, cols] + cw_ref[0:1, cols] * u2 + cw_ref[1:2, cols] * u1
                + cw_ref[2:3, cols] * u)

    def prologue(rows):
        return _adaln(x_ref[0, rows, :], g_ref[2:3, :], shift, scale).astype(BF16)

    subs = [slice(s * sub, (s + 1) * sub) for s in range(tm // sub)]
    for rows in subs:
        h = prologue(rows)
        for c in range(F // FC):
            gate = conv(h, slice(c * FC, (c + 1) * FC))
            val = conv(h, slice(F + c * FC, F + (c + 1) * FC))
            act_ref[rows, c * FC:(c + 1) * FC] = (gate * _sigmoid(gate) * val).astype(BF16)
        y = _dot(act_ref[rows, :], wdn_ref[...])
        o_ref[0, rows, :] = (x_ref[0, rows, :]
                             + (1.0 + gate_mod) * _rmsnorm(y, g_ref[3:4, :]))


def _ffn_layer(x, mod, g, w_up, conv_w, conv_b, w_down, l, tm, sub):
    B, S, Dm = x.shape
    F2 = w_up.shape[-1]
    row_spec = pl.BlockSpec((1, tm, Dm), lambda b, i: (b, i, 0))
    return pl.pallas_call(
        functools.partial(_ffn_kernel, tm=tm, sub=sub),
        grid=(B, S // tm),
        in_specs=[
            row_spec,
            _mod_spec(mod, l),
            _resident_layer(g, l),
            _resident_layer(w_up, l),
            _resident_layer(conv_w, l),
            _resident_layer(conv_b, l),
            _resident_layer(w_down, l),
        ],
        out_specs=row_spec,
        out_shape=jax.ShapeDtypeStruct(x.shape, F32),
        scratch_shapes=[
            pltpu.VMEM((tm, F2 // 2), BF16),
            pltpu.VMEM((V7X_SUBLANES + sub, F2), F32),
        ],
        compiler_params=_params(),
        name="conv_ffn_sublayer",
    )(x, mod, g, w_up, conv_w, conv_b, w_down)


def _ret_kernel(x_ref, mod_ref, g_ref, win_ref, wout_ref, cos_ref, sin_ref, dmask_ref,
                xi_ref, zeta_ref, gch_ref, o_ref, proj_ref, go_ref, state_ref, *, tm):
    H, dk, dv, C = RET_HEADS, RET_QK_DIM, RET_V_DIM, RET_KERNEL_CHUNK
    half = dk // 2
    k_off, v_off, g_off = H * dk, 2 * H * dk, 2 * H * dk + H * dv

    @pl.when(pl.program_id(1) == 0)
    def _():
        state_ref[...] = jnp.zeros_like(state_ref)

    x = x_ref[0]
    shift, scale, gate_mod = mod_ref[0, 0:1, :], mod_ref[0, 1:2, :], mod_ref[0, 2:3, :]
    h = _adaln(x, g_ref[0:1, :], shift, scale).astype(BF16)
    proj_ref[...] = _dot(h, win_ref[...])

    def rotary(t, cos, sin):
        t1, t2 = t[:, :half], t[:, half:]
        return jnp.concatenate([t1 * cos - t2 * sin, t2 * cos + t1 * sin], axis=-1)

    heads = range(H)
    for c in range(tm // C):
        rows = slice(c * C, (c + 1) * C)
        cos, sin = cos_ref[rows, :], sin_ref[rows, :]
        q = [rotary(proj_ref[rows, hd * dk:(hd + 1) * dk], cos, sin).astype(BF16) for hd in heads]
        k = [(rotary(proj_ref[rows, k_off + hd * dk:k_off + (hd + 1) * dk], cos, sin)
              * (dk ** -0.5)).astype(BF16) for hd in heads]
        v = [proj_ref[rows, v_off + hd * dv:v_off + (hd + 1) * dv] for hd in heads]
        s = [_dot_nt(q[hd], k[hd]) for hd in heads]
        s = [(s[hd] * dmask_ref[hd]).astype(BF16) for hd in heads]
        inner = [_dot(s[hd], v[hd].astype(BF16)) for hd in heads]
        state = [state_ref[hd] for hd in heads]
        cross = [_dot(q[hd], state[hd].astype(BF16)) for hd in heads]
        kv = [_dot_tn(k[hd], (v[hd] * zeta_ref[hd]).astype(BF16)) for hd in heads]
        for hd in heads:
            state_ref[hd] = state[hd] * gch_ref[hd] + kv[hd]
        for hd in heads:
            o = inner[hd] + cross[hd] * xi_ref[hd]
            mu = jnp.mean(o, axis=-1, keepdims=True)
            oc = o - mu
            var = jnp.mean(oc * oc, axis=-1, keepdims=True)
            on = oc * lax.rsqrt(var + NORM_EPS)
            gate = proj_ref[rows, g_off + hd * dv:g_off + (hd + 1) * dv]
            go_ref[rows, hd * dv:(hd + 1) * dv] = (gate * _sigmoid(gate) * on).astype(BF16)

    y = _dot(go_ref[...], wout_ref[...])
    o_ref[0] = x + (1.0 + gate_mod) * _rmsnorm(y, g_ref[1:2, :])


def _retention_tables():
    H, C, dv = RET_HEADS, RET_KERNEL_CHUNK, RET_V_DIM
    log_gamma = jnp.log(1.0 - 2.0 ** (-5.0 - jnp.arange(H, dtype=F32)))
    idx = jnp.arange(C, dtype=F32)
    diff = idx[:, None] - idx[None, :]
    dmask = jnp.where(diff[None] >= 0,
                      jnp.exp(jnp.maximum(diff, 0.0)[None] * log_gamma[:, None, None]), 0.0)
    xi = jnp.exp((idx[None, :] + 1.0) * log_gamma[:, None])
    zeta = jnp.exp((C - 1.0 - idx[None, :]) * log_gamma[:, None])
    g_chunk = jnp.exp(C * log_gamma)
    xi_b = jnp.broadcast_to(xi[:, :, None], (H, C, dv))
    zeta_b = jnp.broadcast_to(zeta[:, :, None], (H, C, dv))
    return dmask, xi_b, zeta_b, g_chunk


def _rotary_tables(S):
    half = RET_QK_DIM // 2
    inv = ROPE_BASE ** (-jnp.arange(half, dtype=F32) / half)
    ang = jnp.arange(S).astype(F32)[:, None] * inv[None, :]
    return jnp.cos(ang), jnp.sin(ang)


def _retention_layer(x, mod, g, w_in, w_out, cos, sin, tables, l, li, tm):
    B, S, Dm = x.shape
    H, dk, dv, C = RET_HEADS, RET_QK_DIM, RET_V_DIM, RET_KERNEL_CHUNK
    dmask, xi_b, zeta_b, g_chunk = tables
    row_spec = pl.BlockSpec((1, tm, Dm), lambda b, i: (b, i, 0))
    pos_spec = pl.BlockSpec((tm, dk // 2), lambda b, i: (i, 0))
    return pl.pallas_call(
        functools.partial(_ret_kernel, tm=tm),
        grid=(B, S // tm),
        in_specs=[
            row_spec,
            _mod_spec(mod, l),
            _resident_layer(g, l),
            _resident_layer(w_in, li),
            _resident_layer(w_out, li),
            pos_spec,
            pos_spec,
            _resident((H, C, C)),
            _resident((H, C, dv)),
            _resident((H, C, dv)),
            pl.BlockSpec(memory_space=pltpu.SMEM),
        ],
        out_specs=row_spec,
        out_shape=jax.ShapeDtypeStruct(x.shape, F32),
        scratch_shapes=[
            pltpu.VMEM((tm, w_in.shape[-1]), F32),
            pltpu.VMEM((tm, H * dv), BF16),
            pltpu.VMEM((H, dk, dv), F32),
        ],
        compiler_params=_params(),
        name="retention_sublayer",
    )(x, mod, g, w_in, w_out, cos, sin, dmask, xi_b, zeta_b, g_chunk)


def _kv_kernel(x_ref, mod_ref, g_ref, wk_ref, wvt_ref, k_ref, vt_ref):
    h = _adaln(x_ref[0], g_ref[...], mod_ref[0, 0:1, :], mod_ref[0, 1:2, :]).astype(BF16)
    k_ref[0] = _dot(h, wk_ref[...]).astype(BF16)
    vt_ref[0] = _dot_nt(wvt_ref[...], h).astype(BF16)


def _shared_kv(x, kv_mod, g, wk_pad, wv_t, tm):
    B, S, Dm = x.shape
    nk, nv = wk_pad.shape[1], wv_t.shape[0]
    return pl.pallas_call(
        _kv_kernel,
        grid=(B, S // tm),
        in_specs=[
            pl.BlockSpec((1, tm, Dm), lambda b, i: (b, i, 0)),
            pl.BlockSpec((1, 2, Dm), lambda b, i: (b, 0, 0)),
            _resident((1, Dm)),
            _resident(wk_pad.shape),
            _resident(wv_t.shape),
        ],
        out_specs=[pl.BlockSpec((1, tm, nk), lambda b, i: (b, i, 0)),
                   pl.BlockSpec((1, nv, tm), lambda b, i: (b, 0, i))],
        out_shape=[jax.ShapeDtypeStruct((B, S, nk), BF16),
                   jax.ShapeDtypeStruct((B, nv, S), BF16)],
        compiler_params=_params(),
        name="shared_kv",
    )(x, kv_mod, g, wk_pad, wv_t)


def _bias_kernel(table_ref, bucket_ref, win_ref, o_ref):
    hd = pl.program_id(0)
    bucket = bucket_ref[...]
    acc = jnp.zeros(bucket.shape, F32)
    for b in range(REL_BUCKETS):
        acc = jnp.where(bucket == b, table_ref[b, hd], acc)
    o_ref[0] = jnp.where(win_ref[...] != 0, acc, MASK_NEG)


def _bias_band_t(rel_bias):
    i = np.arange(BLOCK)[None, :]
    j = np.arange(2 * BLOCK)[:, None]
    dist = i + BLOCK - j
    n = np.maximum(dist, 0)
    max_exact = REL_BUCKETS // 2
    large = max_exact + (np.log(np.maximum(n, 1).astype(np.float32) / max_exact)
                         / math.log(REL_MAX_DIST / max_exact)
                         * (REL_BUCKETS - max_exact)).astype(np.int32)
    large = np.minimum(large, REL_BUCKETS - 1)
    bucket = np.where(n < max_exact, n, large).astype(np.int32)
    win = ((dist >= 0) & (dist < WINDOW)).astype(np.int32)
    full = pl.BlockSpec((2 * BLOCK, BLOCK), lambda h: (0, 0))
    return pl.pallas_call(
        _bias_kernel,
        grid=(SWA_HEADS,),
        in_specs=[pl.BlockSpec(memory_space=pltpu.SMEM), full, full],
        out_specs=pl.BlockSpec((1, 2 * BLOCK, BLOCK), lambda h: (h, 0, 0)),
        out_shape=jax.ShapeDtypeStruct((SWA_HEADS, 2 * BLOCK, BLOCK), F32),
        name="rel_bias_band",
    )(rel_bias, jnp.asarray(bucket), jnp.asarray(win))


def _swa_kernel(x_ref, mod_ref, g_ref, wqt_ref, wo_ref, kc_ref, kp_ref, vtc_ref, vtp_ref,
                bias_ref, sink_ref, o_ref, qt_ref, att_ref, *, tm, sub):
    KV, G, hd, KL = SWA_KV_HEADS, SWA_GROUP, SWA_HEAD_DIM, K_PAD_LANES
    n_q = SWA_HEADS * hd
    first_tile = pl.program_id(1) == 0
    shift, scale, gate_mod = mod_ref[0, 0:1, :], mod_ref[0, 1:2, :], mod_ref[0, 2:3, :]

    qt_ref[n_q:n_q + hd, :] = jnp.zeros((hd, tm), BF16)
    key_row = lax.broadcasted_iota(jnp.int32, (2 * BLOCK, GQ_ROWS), 0)
    edge_mask = jnp.where(jnp.logical_and(first_tile, key_row < BLOCK), MASK_NEG, 0.0)

    def scores(j, kh):
        cols = slice(j * BLOCK, (j + 1) * BLOCK)
        klanes = slice(kh * KL, (kh + 1) * KL)
        if j == 0:
            keys = jnp.concatenate([kp_ref[0, :, klanes], kc_ref[0, cols, klanes]], axis=0)
        else:
            keys = kc_ref[0, (j - 1) * BLOCK:(j + 1) * BLOCK, klanes]
        q_cat = jnp.concatenate(
            [qt_ref[(kh * G + g) * hd:(kh * G + g) * hd + KL, cols] for g in range(G)],
            axis=1)
        s = _dot(keys, q_cat) + bias_ref[kh]
        return s + edge_mask if j == 0 else s

    def weighted_values(j, kh, p):
        cols = slice(j * BLOCK, (j + 1) * BLOCK)
        vrows = slice(kh * hd, (kh + 1) * hd)
        if j == 0:
            vals_t = jnp.concatenate([vtp_ref[0, vrows, :], vtc_ref[0, vrows, cols]], axis=1)
        else:
            vals_t = vtc_ref[0, vrows, (j - 1) * BLOCK:(j + 1) * BLOCK]
        o_t = _dot(vals_t, p)
        for g in range(G):
            att_ref[(kh * G + g) * hd:(kh * G + g + 1) * hd, cols] = (
                o_t[:, g * BLOCK:(g + 1) * BLOCK].astype(BF16))

    def attend(blocks):
        slabs = [(j, kh) for j in blocks for kh in range(KV)]
        s = [scores(j, kh) for j, kh in slabs]
        m = [jnp.maximum(jnp.max(si, axis=0, keepdims=True), sink_ref[kh])
             for si, (_, kh) in zip(s, slabs)]
        e = [jnp.exp(si - mi) for si, mi in zip(s, m)]
        inv = [1.0 / (jnp.sum(ei, axis=0, keepdims=True) + jnp.exp(sink_ref[kh] - mi))
               for ei, mi, (_, kh) in zip(e, m, slabs)]
        for ei, ii, (j, kh) in zip(e, inv, slabs):
            weighted_values(j, kh, (ei * ii).astype(BF16))

    h = _adaln(x_ref[0], g_ref[0:1, :], shift, scale).astype(BF16)
    qt_ref[0:n_q, :] = (_dot_nt(wqt_ref[...], h) * (hd ** -0.5)).astype(BF16)
    nb = sub // BLOCK
    for s in range(tm // sub):
        attend(range(s * nb, (s + 1) * nb))
    y = _dot_tn(att_ref[...], wo_ref[...])
    o_ref[0] = x_ref[0] + (1.0 + gate_mod) * _rmsnorm(y, g_ref[1:2, :])


def _swa_layer(x, mod, g, w_q_t, w_o, k_pad, v_t, bias_t, sink_rows, l, li, tm, sub):
    B, S, Dm = x.shape
    nk, nv = k_pad.shape[-1], v_t.shape[1]
    n_q = w_q_t.shape[1]
    nb = tm // BLOCK
    row_spec = pl.BlockSpec((1, tm, Dm), lambda b, i: (b, i, 0))
    prev_blk = lambda i: jnp.maximum(i * nb - 1, 0)
    return pl.pallas_call(
        functools.partial(_swa_kernel, tm=tm, sub=sub),
        grid=(B, S // tm),
        in_specs=[
            row_spec,
            _mod_spec(mod, l),
            _resident_layer(g, l),
            _resident_layer(w_q_t, li),
            _resident_layer(w_o, li),
            pl.BlockSpec((1, tm, nk), lambda b, i: (b, i, 0)),
            pl.BlockSpec((1, BLOCK, nk), lambda b, i: (b, prev_blk(i), 0)),
            pl.BlockSpec((1, nv, tm), lambda b, i: (b, 0, i)),
            pl.BlockSpec((1, nv, BLOCK), lambda b, i: (b, 0, prev_blk(i))),
            _resident(bias_t.shape),
            _resident_layer(sink_rows, li),
        ],
        out_specs=row_spec,
        out_shape=jax.ShapeDtypeStruct(x.shape, F32),
        scratch_shapes=[
            pltpu.VMEM((n_q + SWA_HEAD_DIM, tm), BF16),
            pltpu.VMEM((n_q, tm), BF16),
        ],
        compiler_params=_params(),
        name="swa_sublayer",
    )(x, mod, g, w_q_t, w_o, k_pad, k_pad, v_t, v_t, bias_t, sink_rows)


def _kv_weights(kv_w):
    Dm = kv_w.shape[0]
    n = SWA_KV_HEADS * SWA_HEAD_DIM
    wk = kv_w[:, :n].reshape(Dm, SWA_KV_HEADS, SWA_HEAD_DIM)
    wk = jnp.pad(wk, ((0, 0), (0, 0), (0, K_PAD_LANES - SWA_HEAD_DIM)))
    return wk.reshape(Dm, SWA_KV_HEADS * K_PAD_LANES).astype(BF16), kv_w[:, n:].T.astype(BF16)


def kernel(x, c, norm_g, ada_w, ada_b, ret_w_in, ret_w_out, kv_norm_g, kv_ada_w, kv_ada_b, kv_w,
           swa_w_q, swa_w_o, swa_sinks, rel_bias, ffn_w_up, ffn_conv_w, ffn_conv_b, ffn_w_down):
    B, S, Dm = x.shape
    TM_FFN, SUB_FFN, TM_RET, TM_SWA, SUB_SWA, TM_KV = 512, 256, 256, 512, 256, 512

    c_pad = jnp.zeros((V7X_SUBLANES, Dm), F32).at[:B].set(c)
    mod = _modulation(c_pad, ada_w, ada_b[:, None, :], 2048)
    mod = mod[:, :B].reshape(DEPTH, B, 6, Dm)
    kv_mod = _modulation(c_pad, kv_ada_w[None], kv_ada_b[None, None, :], 2048)
    kv_mod = kv_mod[0, :B].reshape(B, 2, Dm)

    cos, sin = _rotary_tables(S)
    tables = _retention_tables()
    bias_t = _bias_band_t(rel_bias).reshape(SWA_KV_HEADS, SWA_GROUP, 2 * BLOCK, BLOCK)
    bias_t = bias_t.transpose(0, 2, 1, 3).reshape(SWA_KV_HEADS, 2 * BLOCK, GQ_ROWS)
    sink_rows = jnp.repeat(swa_sinks.astype(F32), BLOCK, axis=1).reshape(
        swa_sinks.shape[0], SWA_KV_HEADS, 1, GQ_ROWS)

    ret_w_in, ret_w_out = ret_w_in.astype(BF16), ret_w_out.astype(BF16)
    swa_w_q_t, swa_w_o = swa_w_q.transpose(0, 2, 1).astype(BF16), swa_w_o.astype(BF16)
    ffn_w_up, ffn_w_down = ffn_w_up.astype(BF16), ffn_w_down.astype(BF16)
    ffn_conv_b = ffn_conv_b[:, None, :]

    k_pad = v_t = None
    for l in range(DEPTH):
        if l < N_A_LAYERS:
            x = _retention_layer(x, mod, norm_g, ret_w_in, ret_w_out, cos, sin, tables, l, l, TM_RET)
        else:
            x = _swa_layer(x, mod, norm_g, swa_w_q_t, swa_w_o, k_pad, v_t, bias_t, sink_rows,
                           l, l - N_A_LAYERS, TM_SWA, SUB_SWA)
        x = _ffn_layer(x, mod, norm_g, ffn_w_up, ffn_conv_w, ffn_conv_b, ffn_w_down,
                       l, TM_FFN, SUB_FFN)
        if l == N_A_LAYERS - 1:
            wk_pad, wv_t = _kv_weights(kv_w)
            k_pad, v_t = _shared_kv(x, kv_mod, kv_norm_g[None, :], wk_pad, wv_t, TM_KV)
    return x
```

```python
import functools
import math

import jax
import jax.numpy as jnp
import numpy as np
from jax import lax
from jax.experimental import pallas as pl
from jax.experimental.pallas import tpu as pltpu

D_MODEL = 1024
DEPTH = 4
N_A_LAYERS = DEPTH // 2
RET_HEADS = 4
RET_QK_DIM = D_MODEL // RET_HEADS
RET_V_DIM = 2 * D_MODEL // RET_HEADS
RET_CHUNK = 128
ROPE_BASE = 10000.0
SWA_HEADS = 16
SWA_KV_HEADS = 4
SWA_GROUP = SWA_HEADS // SWA_KV_HEADS
SWA_HEAD_DIM = 64
WINDOW = 128
BLOCK = WINDOW
REL_BUCKETS = 32
REL_MAX_DIST = 128
D_FF = 2816
CONV_WIDTH = 3
NORM_EPS = 1e-6

V7X_SUBLANES = 8
V7X_LANES = 128
V7X_MXU_DIM = 256
V7X_VMEM_LIMIT_BYTES = 58 * 1024 * 1024

MASK_NEG = -1e30
BF16 = jnp.bfloat16
F32 = jnp.float32

RET_KERNEL_CHUNK = V7X_MXU_DIM
FFN_COL_CHUNK = V7X_MXU_DIM
GQ_ROWS = SWA_GROUP * BLOCK
K_PAD_LANES = V7X_LANES


def _dot(a, b):
    return jnp.dot(a, b, preferred_element_type=F32)


def _dot_nt(a, b):
    return lax.dot_general(a, b, (((1,), (1,)), ((), ())), preferred_element_type=F32)


def _dot_tn(a, b):
    return lax.dot_general(a, b, (((0,), (0,)), ((), ())), preferred_element_type=F32)


def _sigmoid(x):
    return 1.0 / (1.0 + jnp.exp(-x))


def _adaln(x, g, shift, scale):
    ms = jnp.mean(x * x, axis=-1, keepdims=True)
    y = x * lax.rsqrt(ms + NORM_EPS) * g
    return y * (1.0 + scale) + shift


def _rmsnorm(y, g):
    ms = jnp.mean(y * y, axis=-1, keepdims=True)
    return y * lax.rsqrt(ms + NORM_EPS) * g


def _resident(shape):
    zeros = (0,) * len(shape)
    return pl.BlockSpec(shape, lambda b, i: zeros, pipeline_mode=pl.Buffered(1))


def _resident_layer(stacked, l):
    tail = (0,) * (stacked.ndim - 1)
    return pl.BlockSpec((None,) + stacked.shape[1:], lambda b, i: (l,) + tail,
                        pipeline_mode=pl.Buffered(1))


def _mod_spec(mod, l):
    return pl.BlockSpec((None, 1) + mod.shape[2:], lambda b, i: (l, b, 0, 0))


def _params():
    return pltpu.CompilerParams(
        dimension_semantics=("arbitrary", "arbitrary"),
        vmem_limit_bytes=V7X_VMEM_LIMIT_BYTES)


def _mod_kernel(c_ref, w_ref, b_ref, o_ref):
    c = c_ref[...]
    ca = (c * _sigmoid(c)).astype(BF16)
    o_ref[0] = _dot(ca, w_ref[0].astype(BF16)) + b_ref[0]


def _modulation(c_pad, w, b, tn):
    L, Dm, N = w.shape
    return pl.pallas_call(
        _mod_kernel,
        grid=(L, N // tn),
        in_specs=[
            pl.BlockSpec((V7X_SUBLANES, Dm), lambda l, j: (0, 0)),
            pl.BlockSpec((1, Dm, tn), lambda l, j: (l, 0, j)),
            pl.BlockSpec((1, 1, tn), lambda l, j: (l, 0, j)),
        ],
        out_specs=pl.BlockSpec((1, V7X_SUBLANES, tn), lambda l, j: (l, 0, j)),
        out_shape=jax.ShapeDtypeStruct((L, V7X_SUBLANES, N), F32),
        compiler_params=_params(),
        name="adaln_modulation",
    )(c_pad, w, b)


def _ffn_kernel(x_ref, mod_ref, g_ref, wup_ref, cw_ref, cb_ref, wdn_ref, o_ref,
                u_ref, act_ref, carry_ref, *, tm, sub):
    F = D_FF
    FC = FFN_COL_CHUNK

    @pl.when(pl.program_id(1) == 0)
    def _():
        carry_ref[...] = jnp.zeros_like(carry_ref)

    shift, scale, gate_mod = mod_ref[0, 3:4, :], mod_ref[0, 4:5, :], mod_ref[0, 5:6, :]
    row = lax.broadcasted_iota(jnp.int32, (sub, FC), 0)

    def conv(rows, cols):
        u = u_ref[rows, cols]
        prev1 = carry_ref[V7X_SUBLANES - 1:V7X_SUBLANES, cols]
        prev2 = carry_ref[V7X_SUBLANES - 2:V7X_SUBLANES - 1, cols]
        u1 = jnp.where(row == 0, prev1, pltpu.roll(u, 1, 0))
        u2 = jnp.where(row == 0, prev2, jnp.where(row == 1, prev1, pltpu.roll(u, 2, 0)))
        return (cb_ref[0:1, cols] + cw_ref[0:1, cols] * u2 + cw_ref[1:2, cols] * u1
                + cw_ref[2:3, cols] * u)

    for s in range(tm // sub):
        rows = slice(s * sub, (s + 1) * sub)
        x = x_ref[0, rows, :]
        h = _adaln(x, g_ref[2:3, :], shift, scale).astype(BF16)
        u_ref[rows, :] = _dot(h, wup_ref[...])
        for c in range(F // FC):
            gate = conv(rows, slice(c * FC, (c + 1) * FC))
            val = conv(rows, slice(F + c * FC, F + (c + 1) * FC))
            act_ref[rows, c * FC:(c + 1) * FC] = (gate * _sigmoid(gate) * val).astype(BF16)
        carry_ref[...] = u_ref[(s + 1) * sub - V7X_SUBLANES:(s + 1) * sub, :]
        y = _dot(act_ref[rows, :], wdn_ref[...])
        o_ref[0, rows, :] = x + (1.0 + gate_mod) * _rmsnorm(y, g_ref[3:4, :])


def _ffn_layer(x, mod, g, w_up, conv_w, conv_b, w_down, l, tm, sub):
    B, S, Dm = x.shape
    F2 = w_up.shape[-1]
    row_spec = pl.BlockSpec((1, tm, Dm), lambda b, i: (b, i, 0))
    return pl.pallas_call(
        functools.partial(_ffn_kernel, tm=tm, sub=sub),
        grid=(B, S // tm),
        in_specs=[
            row_spec,
            _mod_spec(mod, l),
            _resident_layer(g, l),
            _resident_layer(w_up, l),
            _resident_layer(conv_w, l),
            _resident_layer(conv_b, l),
            _resident_layer(w_down, l),
        ],
        out_specs=row_spec,
        out_shape=jax.ShapeDtypeStruct(x.shape, F32),
        scratch_shapes=[
            pltpu.VMEM((tm, F2), F32),
            pltpu.VMEM((tm, F2 // 2), BF16),
            pltpu.VMEM((V7X_SUBLANES, F2), F32),
        ],
        compiler_params=_params(),
        name="conv_ffn_sublayer",
    )(x, mod, g, w_up, conv_w, conv_b, w_down)


def _ret_kernel(x_ref, mod_ref, g_ref, win_ref, wout_ref, cos_ref, sin_ref, dmask_ref,
                xi_ref, zeta_ref, gch_ref, o_ref, proj_ref, go_ref, state_ref, *, tm):
    H, dk, dv, C = RET_HEADS, RET_QK_DIM, RET_V_DIM, RET_KERNEL_CHUNK
    half = dk // 2
    k_off, v_off, g_off = H * dk, 2 * H * dk, 2 * H * dk + H * dv

    @pl.when(pl.program_id(1) == 0)
    def _():
        state_ref[...] = jnp.zeros_like(state_ref)

    shift, scale, gate_mod = mod_ref[0, 0:1, :], mod_ref[0, 1:2, :], mod_ref[0, 2:3, :]

    def rotary(t, cos, sin):
        t1, t2 = t[:, :half], t[:, half:]
        return jnp.concatenate([t1 * cos - t2 * sin, t2 * cos + t1 * sin], axis=-1)

    def project(rows):
        h = _adaln(x_ref[0, rows, :], g_ref[0:1, :], shift, scale).astype(BF16)
        proj_ref[rows, :] = _dot(h, win_ref[...])

    heads = range(H)

    def core(rows):
        cos, sin = cos_ref[rows, :], sin_ref[rows, :]
        q = [rotary(proj_ref[rows, hd * dk:(hd + 1) * dk], cos, sin).astype(BF16) for hd in heads]
        k = [(rotary(proj_ref[rows, k_off + hd * dk:k_off + (hd + 1) * dk], cos, sin)
              * (dk ** -0.5)).astype(BF16) for hd in heads]
        v = [proj_ref[rows, v_off + hd * dv:v_off + (hd + 1) * dv] for hd in heads]
        s = [_dot_nt(q[hd], k[hd]) for hd in heads]
        s = [(s[hd] * dmask_ref[hd]).astype(BF16) for hd in heads]
        inner = [_dot(s[hd], v[hd].astype(BF16)) for hd in heads]
        state = [state_ref[hd] for hd in heads]
        cross = [_dot(q[hd], state[hd].astype(BF16)) for hd in heads]
        kv = [_dot_tn(k[hd], (v[hd] * zeta_ref[hd]).astype(BF16)) for hd in heads]
        for hd in heads:
            state_ref[hd] = state[hd] * gch_ref[hd] + kv[hd]
        for hd in heads:
            o = inner[hd] + cross[hd] * xi_ref[hd]
            mu = jnp.mean(o, axis=-1, keepdims=True)
            oc = o - mu
            var = jnp.mean(oc * oc, axis=-1, keepdims=True)
            on = oc * lax.rsqrt(var + NORM_EPS)
            gate = proj_ref[rows, g_off + hd * dv:g_off + (hd + 1) * dv]
            go_ref[rows, hd * dv:(hd + 1) * dv] = (gate * _sigmoid(gate) * on).astype(BF16)

    def finish(rows):
        y = _dot(go_ref[rows, :], wout_ref[...])
        o_ref[0, rows, :] = (x_ref[0, rows, :]
                             + (1.0 + gate_mod) * _rmsnorm(y, g_ref[1:2, :]))

    chunks = [slice(c * C, (c + 1) * C) for c in range(tm // C)]
    for rows in chunks:
        project(rows)
    for rows in chunks:
        core(rows)
    for rows in chunks:
        finish(rows)


def _retention_tables():
    H, C, dv = RET_HEADS, RET_KERNEL_CHUNK, RET_V_DIM
    log_gamma = jnp.log(1.0 - 2.0 ** (-5.0 - jnp.arange(H, dtype=F32)))
    idx = jnp.arange(C, dtype=F32)
    diff = idx[:, None] - idx[None, :]
    dmask = jnp.where(diff[None] >= 0,
                      jnp.exp(jnp.maximum(diff, 0.0)[None] * log_gamma[:, None, None]), 0.0)
    xi = jnp.exp((idx[None, :] + 1.0) * log_gamma[:, None])
    zeta = jnp.exp((C - 1.0 - idx[None, :]) * log_gamma[:, None])
    g_chunk = jnp.exp(C * log_gamma)
    xi_b = jnp.broadcast_to(xi[:, :, None], (H, C, dv))
    zeta_b = jnp.broadcast_to(zeta[:, :, None], (H, C, dv))
    return dmask, xi_b, zeta_b, g_chunk


def _rotary_tables(S):
    half = RET_QK_DIM // 2
    inv = ROPE_BASE ** (-jnp.arange(half, dtype=F32) / half)
    ang = jnp.arange(S).astype(F32)[:, None] * inv[None, :]
    return jnp.cos(ang), jnp.sin(ang)


def _retention_layer(x, mod, g, w_in, w_out, cos, sin, tables, l, li, tm):
    B, S, Dm = x.shape
    H, dk, dv, C = RET_HEADS, RET_QK_DIM, RET_V_DIM, RET_KERNEL_CHUNK
    dmask, xi_b, zeta_b, g_chunk = tables
    row_spec = pl.BlockSpec((1, tm, Dm), lambda b, i: (b, i, 0))
    pos_spec = pl.BlockSpec((tm, dk // 2), lambda b, i: (i, 0))
    return pl.pallas_call(
        functools.partial(_ret_kernel, tm=tm),
        grid=(B, S // tm),
        in_specs=[
            row_spec,
            _mod_spec(mod, l),
            _resident_layer(g, l),
            _resident_layer(w_in, li),
            _resident_layer(w_out, li),
            pos_spec,
            pos_spec,
            _resident((H, C, C)),
            _resident((H, C, dv)),
            _resident((H, C, dv)),
            pl.BlockSpec(memory_space=pltpu.SMEM),
        ],
        out_specs=row_spec,
        out_shape=jax.ShapeDtypeStruct(x.shape, F32),
        scratch_shapes=[
            pltpu.VMEM((tm, w_in.shape[-1]), F32),
            pltpu.VMEM((tm, H * dv), BF16),
            pltpu.VMEM((H, dk, dv), F32),
        ],
        compiler_params=_params(),
        name="retention_sublayer",
    )(x, mod, g, w_in, w_out, cos, sin, dmask, xi_b, zeta_b, g_chunk)


def _kv_kernel(x_ref, mod_ref, g_ref, wk_ref, wvt_ref, k_ref, vt_ref):
    h = _adaln(x_ref[0], g_ref[...], mod_ref[0, 0:1, :], mod_ref[0, 1:2, :]).astype(BF16)
    k_ref[0] = _dot(h, wk_ref[...]).astype(BF16)
    vt_ref[0] = _dot_nt(wvt_ref[...], h).astype(BF16)


def _shared_kv(x, kv_mod, g, wk_pad, wv_t, tm):
    B, S, Dm = x.shape
    nk, nv = wk_pad.shape[1], wv_t.shape[0]
    return pl.pallas_call(
        _kv_kernel,
        grid=(B, S // tm),
        in_specs=[
            pl.BlockSpec((1, tm, Dm), lambda b, i: (b, i, 0)),
            pl.BlockSpec((1, 2, Dm), lambda b, i: (b, 0, 0)),
            _resident((1, Dm)),
            _resident(wk_pad.shape),
            _resident(wv_t.shape),
        ],
        out_specs=[pl.BlockSpec((1, tm, nk), lambda b, i: (b, i, 0)),
                   pl.BlockSpec((1, nv, tm), lambda b, i: (b, 0, i))],
        out_shape=[jax.ShapeDtypeStruct((B, S, nk), BF16),
                   jax.ShapeDtypeStruct((B, nv, S), BF16)],
        compiler_params=_params(),
        name="shared_kv",
    )(x, kv_mod, g, wk_pad, wv_t)


def _bias_kernel(table_ref, bucket_ref, win_ref, o_ref):
    hd = pl.program_id(0)
    bucket = bucket_ref[...]
    acc = jnp.zeros(bucket.shape, F32)
    for b in range(REL_BUCKETS):
        acc = jnp.where(bucket == b, table_ref[b, hd], acc)
    o_ref[0] = jnp.where(win_ref[...] != 0, acc, MASK_NEG)


def _bias_band_t(rel_bias):
    i = np.arange(BLOCK)[None, :]
    j = np.arange(2 * BLOCK)[:, None]
    dist = i + BLOCK - j
    n = np.maximum(dist, 0)
    max_exact = REL_BUCKETS // 2
    large = max_exact + (np.log(np.maximum(n, 1).astype(np.float32) / max_exact)
                         / math.log(REL_MAX_DIST / max_exact)
                         * (REL_BUCKETS - max_exact)).astype(np.int32)
    large = np.minimum(large, REL_BUCKETS - 1)
    bucket = np.where(n < max_exact, n, large).astype(np.int32)
    win = ((dist >= 0) & (dist < WINDOW)).astype(np.int32)
    full = pl.BlockSpec((2 * BLOCK, BLOCK), lambda h: (0, 0))
    return pl.pallas_call(
        _bias_kernel,
        grid=(SWA_HEADS,),
        in_specs=[pl.BlockSpec(memory_space=pltpu.SMEM), full, full],
        out_specs=pl.BlockSpec((1, 2 * BLOCK, BLOCK), lambda h: (h, 0, 0)),
        out_shape=jax.ShapeDtypeStruct((SWA_HEADS, 2 * BLOCK, BLOCK), F32),
        name="rel_bias_band",
    )(rel_bias, jnp.asarray(bucket), jnp.asarray(win))


def _swa_kernel(x_ref, mod_ref, g_ref, wqt_ref, wo_ref, kc_ref, kp_ref, vtc_ref, vtp_ref,
                bias_ref, sink_ref, o_ref, qt_ref, att_ref, *, tm, sub):
    KV, G, hd, KL = SWA_KV_HEADS, SWA_GROUP, SWA_HEAD_DIM, K_PAD_LANES
    n_q = SWA_HEADS * hd
    first_tile = pl.program_id(1) == 0
    shift, scale, gate_mod = mod_ref[0, 0:1, :], mod_ref[0, 1:2, :], mod_ref[0, 2:3, :]

    qt_ref[n_q:n_q + hd, :] = jnp.zeros((hd, tm), BF16)
    key_row = lax.broadcasted_iota(jnp.int32, (2 * BLOCK, GQ_ROWS), 0)
    edge_mask = jnp.where(jnp.logical_and(first_tile, key_row < BLOCK), MASK_NEG, 0.0)

    def scores(j, kh):
        cols = slice(j * BLOCK, (j + 1) * BLOCK)
        klanes = slice(kh * KL, (kh + 1) * KL)
        if j == 0:
            keys = jnp.concatenate([kp_ref[0, :, klanes], kc_ref[0, cols, klanes]], axis=0)
        else:
            keys = kc_ref[0, (j - 1) * BLOCK:(j + 1) * BLOCK, klanes]
        q_cat = jnp.concatenate(
            [qt_ref[(kh * G + g) * hd:(kh * G + g) * hd + KL, cols] for g in range(G)],
            axis=1)
        s = _dot(keys, q_cat) + bias_ref[kh]
        return s + edge_mask if j == 0 else s

    def weighted_values(j, kh, p):
        cols = slice(j * BLOCK, (j + 1) * BLOCK)
        vrows = slice(kh * hd, (kh + 1) * hd)
        if j == 0:
            vals_t = jnp.concatenate([vtp_ref[0, vrows, :], vtc_ref[0, vrows, cols]], axis=1)
        else:
            vals_t = vtc_ref[0, vrows, (j - 1) * BLOCK:(j + 1) * BLOCK]
        o_t = _dot(vals_t, p)
        for g in range(G):
            att_ref[(kh * G + g) * hd:(kh * G + g + 1) * hd, cols] = (
                o_t[:, g * BLOCK:(g + 1) * BLOCK].astype(BF16))

    def attend(blocks):
        slabs = [(j, kh) for j in blocks for kh in range(KV)]
        s = [scores(j, kh) for j, kh in slabs]
        m = [jnp.maximum(jnp.max(si, axis=0, keepdims=True), sink_ref[kh])
             for si, (_, kh) in zip(s, slabs)]
        e = [jnp.exp(si - mi) for si, mi in zip(s, m)]
        inv = [1.0 / (jnp.sum(ei, axis=0, keepdims=True) + jnp.exp(sink_ref[kh] - mi))
               for ei, mi, (_, kh) in zip(e, m, slabs)]
        for ei, ii, (j, kh) in zip(e, inv, slabs):
            weighted_values(j, kh, (ei * ii).astype(BF16))

    h = _adaln(x_ref[0], g_ref[0:1, :], shift, scale).astype(BF16)
    qt_ref[0:n_q, :] = (_dot_nt(wqt_ref[...], h) * (hd ** -0.5)).astype(BF16)
    nb = sub // BLOCK
    for s in range(tm // sub):
        attend(range(s * nb, (s + 1) * nb))
    y = _dot_tn(att_ref[...], wo_ref[...])
    o_ref[0] = x_ref[0] + (1.0 + gate_mod) * _rmsnorm(y, g_ref[1:2, :])


def _swa_layer(x, mod, g, w_q_t, w_o, k_pad, v_t, bias_t, sink_rows, l, li, tm, sub):
    B, S, Dm = x.shape
    nk, nv = k_pad.shape[-1], v_t.shape[1]
    n_q = w_q_t.shape[1]
    nb = tm // BLOCK
    row_spec = pl.BlockSpec((1, tm, Dm), lambda b, i: (b, i, 0))
    prev_blk = lambda i: jnp.maximum(i * nb - 1, 0)
    return pl.pallas_call(
        functools.partial(_swa_kernel, tm=tm, sub=sub),
        grid=(B, S // tm),
        in_specs=[
            row_spec,
            _mod_spec(mod, l),
            _resident_layer(g, l),
            _resident_layer(w_q_t, li),
            _resident_layer(w_o, li),
            pl.BlockSpec((1, tm, nk), lambda b, i: (b, i, 0)),
            pl.BlockSpec((1, BLOCK, nk), lambda b, i: (b, prev_blk(i), 0)),
            pl.BlockSpec((1, nv, tm), lambda b, i: (b, 0, i)),
            pl.BlockSpec((1, nv, BLOCK), lambda b, i: (b, 0, prev_blk(i))),
            _resident(bias_t.shape),
            _resident_layer(sink_rows, li),
        ],
        out_specs=row_spec,
        out_shape=jax.ShapeDtypeStruct(x.shape, F32),
        scratch_shapes=[
            pltpu.VMEM((n_q + SWA_HEAD_DIM, tm), BF16),
            pltpu.VMEM((n_q, tm), BF16),
        ],
        compiler_params=_params(),
        name="swa_sublayer",
    )(x, mod, g, w_q_t, w_o, k_pad, k_pad, v_t, v_t, bias_t, sink_rows)


def _kv_weights(kv_w):
    Dm = kv_w.shape[0]
    n = SWA_KV_HEADS * SWA_HEAD_DIM
    wk = kv_w[:, :n].reshape(Dm, SWA_KV_HEADS, SWA_HEAD_DIM)
    wk = jnp.pad(wk, ((0, 0), (0, 0), (0, K_PAD_LANES - SWA_HEAD_DIM)))
    return wk.reshape(Dm, SWA_KV_HEADS * K_PAD_LANES).astype(BF16), kv_w[:, n:].T.astype(BF16)


def kernel(x, c, norm_g, ada_w, ada_b, ret_w_in, ret_w_out, kv_norm_g, kv_ada_w, kv_ada_b, kv_w,
           swa_w_q, swa_w_o, swa_sinks, rel_bias, ffn_w_up, ffn_conv_w, ffn_conv_b, ffn_w_down):
    B, S, Dm = x.shape
    TM_FFN, SUB_FFN, TM_RET, TM_SWA, SUB_SWA, TM_KV = 256, 256, 512, 512, 256, 512

    c_pad = jnp.zeros((V7X_SUBLANES, Dm), F32).at[:B].set(c)
    mod = _modulation(c_pad, ada_w, ada_b[:, None, :], 2048)
    mod = mod[:, :B].reshape(DEPTH, B, 6, Dm)
    kv_mod = _modulation(c_pad, kv_ada_w[None], kv_ada_b[None, None, :], 2048)
    kv_mod = kv_mod[0, :B].reshape(B, 2, Dm)

    cos, sin = _rotary_tables(S)
    tables = _retention_tables()
    bias_t = _bias_band_t(rel_bias).reshape(SWA_KV_HEADS, SWA_GROUP, 2 * BLOCK, BLOCK)
    bias_t = bias_t.transpose(0, 2, 1, 3).reshape(SWA_KV_HEADS, 2 * BLOCK, GQ_ROWS)
    sink_rows = jnp.repeat(swa_sinks.astype(F32), BLOCK, axis=1).reshape(
        swa_sinks.shape[0], SWA_KV_HEADS, 1, GQ_ROWS)

    ret_w_in, ret_w_out = ret_w_in.astype(BF16), ret_w_out.astype(BF16)
    swa_w_q_t, swa_w_o = swa_w_q.transpose(0, 2, 1).astype(BF16), swa_w_o.astype(BF16)
    ffn_w_up, ffn_w_down = ffn_w_up.astype(BF16), ffn_w_down.astype(BF16)
    ffn_conv_b = ffn_conv_b[:, None, :]

    k_pad = v_t = None
    for l in range(DEPTH):
        if l < N_A_LAYERS:
            x = _retention_layer(x, mod, norm_g, ret_w_in, ret_w_out, cos, sin, tables, l, l, TM_RET)
        else:
            x = _swa_layer(x, mod, norm_g, swa_w_q_t, swa_w_o, k_pad, v_t, bias_t, sink_rows,
                           l, l - N_A_LAYERS, TM_SWA, SUB_SWA)
        x = _ffn_layer(x, mod, norm_g, ffn_w_up, ffn_conv_w, ffn_conv_b, ffn_w_down,
                       l, TM_FFN, SUB_FFN)
        if l == N_A_LAYERS - 1:
            wk_pad, wv_t = _kv_weights(kv_w)
            k_pad, v_t = _shared_kv(x, kv_mod, kv_norm_g[None, :], wk_pad, wv_t, TM_KV)
    return x
```

```python
import functools
import math

import jax
import jax.numpy as jnp
import numpy as np
from jax import lax
from jax.experimental import pallas as pl
from jax.experimental.pallas import tpu as pltpu

D_MODEL = 1024
DEPTH = 4
N_A_LAYERS = DEPTH // 2
RET_HEADS = 4
RET_QK_DIM = D_MODEL // RET_HEADS
RET_V_DIM = 2 * D_MODEL // RET_HEADS
RET_CHUNK = 128
ROPE_BASE = 10000.0
SWA_HEADS = 16
SWA_KV_HEADS = 4
SWA_GROUP = SWA_HEADS // SWA_KV_HEADS
SWA_HEAD_DIM = 64
WINDOW = 128
BLOCK = WINDOW
REL_BUCKETS = 32
REL_MAX_DIST = 128
D_FF = 2816
CONV_WIDTH = 3
NORM_EPS = 1e-6

V7X_SUBLANES = 8
V7X_LANES = 128
V7X_MXU_DIM = 256
V7X_VMEM_LIMIT_BYTES = 58 * 1024 * 1024

MASK_NEG = -1e30
BF16 = jnp.bfloat16
F32 = jnp.float32

RET_KERNEL_CHUNK = V7X_MXU_DIM
FFN_COL_CHUNK = V7X_MXU_DIM
GQ_ROWS = SWA_GROUP * BLOCK
K_PAD_LANES = V7X_LANES

def _dot(a, b):
    return jnp.dot(a, b, preferred_element_type=F32)


def _dot_nt(a, b):
    return lax.dot_general(a, b, (((1,), (1,)), ((), ())), preferred_element_type=F32)


def _dot_tn(a, b):
    return lax.dot_general(a, b, (((0,), (0,)), ((), ())), preferred_element_type=F32)


def _sigmoid(x):
    return 1.0 / (1.0 + jnp.exp(-x))


def _adaln(x, g, shift, scale):
    ms = jnp.mean(x * x, axis=-1, keepdims=True)
    y = x * lax.rsqrt(ms + NORM_EPS) * g
    return y * (1.0 + scale) + shift


def _rmsnorm(y, g):
    ms = jnp.mean(y * y, axis=-1, keepdims=True)
    return y * lax.rsqrt(ms + NORM_EPS) * g


def _resident(shape):
    zeros = (0,) * len(shape)
    return pl.BlockSpec(shape, lambda b, i: zeros, pipeline_mode=pl.Buffered(1))


def _resident_layer(stacked, l):
    tail = (0,) * (stacked.ndim - 1)
    return pl.BlockSpec((None,) + stacked.shape[1:], lambda b, i: (l,) + tail,
                        pipeline_mode=pl.Buffered(1))


def _mod_spec(mod, l):
    return pl.BlockSpec((None, 1) + mod.shape[2:], lambda b, i: (l, b, 0, 0))


def _params():
    return pltpu.CompilerParams(
        dimension_semantics=("arbitrary", "arbitrary"),
        vmem_limit_bytes=V7X_VMEM_LIMIT_BYTES)


def _mod_kernel(c_ref, w_ref, b_ref, o_ref):
    c = c_ref[...]
    ca = (c * _sigmoid(c)).astype(BF16)
    o_ref[0] = _dot(ca, w_ref[0].astype(BF16)) + b_ref[0]


def _modulation(c_pad, w, b, tn):
    L, Dm, N = w.shape
    return pl.pallas_call(
        _mod_kernel,
        grid=(L, N // tn),
        in_specs=[
            pl.BlockSpec((V7X_SUBLANES, Dm), lambda l, j: (0, 0)),
            pl.BlockSpec((1, Dm, tn), lambda l, j: (l, 0, j)),
            pl.BlockSpec((1, 1, tn), lambda l, j: (l, 0, j)),
        ],
        out_specs=pl.BlockSpec((1, V7X_SUBLANES, tn), lambda l, j: (l, 0, j)),
        out_shape=jax.ShapeDtypeStruct((L, V7X_SUBLANES, N), F32),
        compiler_params=_params(),
        name="adaln_modulation",
    )(c_pad, w, b)


def _ffn_kernel(x_ref, mod_ref, g_ref, wup_ref, cw_ref, cb_ref, wdn_ref, o_ref,
                u_ref, act_ref, carry_ref, *, tm, sub):
    F = D_FF
    FC = FFN_COL_CHUNK

    @pl.when(pl.program_id(1) == 0)
    def _():
        carry_ref[...] = jnp.zeros_like(carry_ref)

    shift, scale, gate_mod = mod_ref[0, 3:4, :], mod_ref[0, 4:5, :], mod_ref[0, 5:6, :]
    row = lax.broadcasted_iota(jnp.int32, (sub, FC), 0)

    def conv(slot, cols):
        u = u_ref[slot, :, cols]
        prev1 = carry_ref[V7X_SUBLANES - 1:V7X_SUBLANES, cols]
        prev2 = carry_ref[V7X_SUBLANES - 2:V7X_SUBLANES - 1, cols]
        u1 = jnp.where(row == 0, prev1, pltpu.roll(u, 1, 0))
        u2 = jnp.where(row == 0, prev2, jnp.where(row == 1, prev1, pltpu.roll(u, 2, 0)))
        return (cb_ref[0:1, cols] + cw_ref[0:1, cols] * u2 + cw_ref[1:2, cols] * u1
                + cw_ref[2:3, cols] * u)

    def up(s):
        rows = slice(s * sub, (s + 1) * sub)
        h = _adaln(x_ref[0, rows, :], g_ref[2:3, :], shift, scale).astype(BF16)
        u_ref[s % 2] = _dot(h, wup_ref[...])

    def activate(s):
        slot = s % 2
        for c in range(F // FC):
            gate = conv(slot, slice(c * FC, (c + 1) * FC))
            val = conv(slot, slice(F + c * FC, F + (c + 1) * FC))
            act_ref[slot, :, c * FC:(c + 1) * FC] = (gate * _sigmoid(gate) * val).astype(BF16)
        carry_ref[...] = u_ref[slot, sub - V7X_SUBLANES:sub, :]

    def down(s):
        rows = slice(s * sub, (s + 1) * sub)
        y = _dot(act_ref[s % 2], wdn_ref[...])
        o_ref[0, rows, :] = (x_ref[0, rows, :]
                             + (1.0 + gate_mod) * _rmsnorm(y, g_ref[3:4, :]))

    n_sub = tm // sub
    up(0)
    for s in range(n_sub):
        if s + 1 < n_sub:
            up(s + 1)
        activate(s)
        down(s)


def _ffn_layer(x, mod, g, w_up, conv_w, conv_b, w_down, l, tm, sub):
    B, S, Dm = x.shape
    F2 = w_up.shape[-1]
    row_spec = pl.BlockSpec((1, tm, Dm), lambda b, i: (b, i, 0))
    return pl.pallas_call(
        functools.partial(_ffn_kernel, tm=tm, sub=sub),
        grid=(B, S // tm),
        in_specs=[
            row_spec,
            _mod_spec(mod, l),
            _resident_layer(g, l),
            _resident_layer(w_up, l),
            _resident_layer(conv_w, l),
            _resident_layer(conv_b, l),
            _resident_layer(w_down, l),
        ],
        out_specs=row_spec,
        out_shape=jax.ShapeDtypeStruct(x.shape, F32),
        scratch_shapes=[
            pltpu.VMEM((2, sub, F2), F32),
            pltpu.VMEM((2, sub, F2 // 2), BF16),
            pltpu.VMEM((V7X_SUBLANES, F2), F32),
        ],
        compiler_params=_params(),
        name="conv_ffn_sublayer",
    )(x, mod, g, w_up, conv_w, conv_b, w_down)


def _ret_kernel(x_ref, mod_ref, g_ref, win_ref, wout_ref, cos_ref, sin_ref, dmask_ref,
                xi_ref, zeta_ref, gch_ref, o_ref, proj_ref, go_ref, state_ref, *, tm):
    H, dk, dv, C = RET_HEADS, RET_QK_DIM, RET_V_DIM, RET_KERNEL_CHUNK
    half = dk // 2
    k_off, v_off, g_off = H * dk, 2 * H * dk, 2 * H * dk + H * dv

    @pl.when(pl.program_id(1) == 0)
    def _():
        state_ref[...] = jnp.zeros_like(state_ref)

    shift, scale, gate_mod = mod_ref[0, 0:1, :], mod_ref[0, 1:2, :], mod_ref[0, 2:3, :]

    def rotary(t, cos, sin):
        t1, t2 = t[:, :half], t[:, half:]
        return jnp.concatenate([t1 * cos - t2 * sin, t2 * cos + t1 * sin], axis=-1)

    def project(rows):
        h = _adaln(x_ref[0, rows, :], g_ref[0:1, :], shift, scale).astype(BF16)
        proj_ref[rows, :] = _dot(h, win_ref[...])

    heads = range(H)

    def core(rows):
        cos, sin = cos_ref[rows, :], sin_ref[rows, :]
        q = [rotary(proj_ref[rows, hd * dk:(hd + 1) * dk], cos, sin).astype(BF16) for hd in heads]
        k = [(rotary(proj_ref[rows, k_off + hd * dk:k_off + (hd + 1) * dk], cos, sin)
              * (dk ** -0.5)).astype(BF16) for hd in heads]
        v = [proj_ref[rows, v_off + hd * dv:v_off + (hd + 1) * dv] for hd in heads]
        s = [_dot_nt(q[hd], k[hd]) for hd in heads]
        s = [(s[hd] * dmask_ref[hd]).astype(BF16) for hd in heads]
        inner = [_dot(s[hd], v[hd].astype(BF16)) for hd in heads]
        state = [state_ref[hd] for hd in heads]
        cross = [_dot(q[hd], state[hd].astype(BF16)) for hd in heads]
        kv = [_dot_tn(k[hd], (v[hd] * zeta_ref[hd]).astype(BF16)) for hd in heads]
        for hd in heads:
            state_ref[hd] = state[hd] * gch_ref[hd] + kv[hd]
        for hd in heads:
            o = inner[hd] + cross[hd] * xi_ref[hd]
            mu = jnp.mean(o, axis=-1, keepdims=True)
            oc = o - mu
            var = jnp.mean(oc * oc, axis=-1, keepdims=True)
            on = oc * lax.rsqrt(var + NORM_EPS)
            gate = proj_ref[rows, g_off + hd * dv:g_off + (hd + 1) * dv]
            go_ref[rows, hd * dv:(hd + 1) * dv] = (gate * _sigmoid(gate) * on).astype(BF16)

    def finish(rows):
        y = _dot(go_ref[rows, :], wout_ref[...])
        o_ref[0, rows, :] = (x_ref[0, rows, :]
                             + (1.0 + gate_mod) * _rmsnorm(y, g_ref[1:2, :]))

    chunks = [slice(c * C, (c + 1) * C) for c in range(tm // C)]
    for rows in chunks:
        project(rows)
    for rows in chunks:
        core(rows)
    for rows in chunks:
        finish(rows)


def _retention_tables():
    H, C, dv = RET_HEADS, RET_KERNEL_CHUNK, RET_V_DIM
    log_gamma = jnp.log(1.0 - 2.0 ** (-5.0 - jnp.arange(H, dtype=F32)))
    idx = jnp.arange(C, dtype=F32)
    diff = idx[:, None] - idx[None, :]
    dmask = jnp.where(diff[None] >= 0,
                      jnp.exp(jnp.maximum(diff, 0.0)[None] * log_gamma[:, None, None]), 0.0)
    xi = jnp.exp((idx[None, :] + 1.0) * log_gamma[:, None])
    zeta = jnp.exp((C - 1.0 - idx[None, :]) * log_gamma[:, None])
    g_chunk = jnp.exp(C * log_gamma)
    xi_b = jnp.broadcast_to(xi[:, :, None], (H, C, dv))
    zeta_b = jnp.broadcast_to(zeta[:, :, None], (H, C, dv))
    return dmask, xi_b, zeta_b, g_chunk


def _rotary_tables(S):
    half = RET_QK_DIM // 2
    inv = ROPE_BASE ** (-jnp.arange(half, dtype=F32) / half)
    ang = jnp.arange(S).astype(F32)[:, None] * inv[None, :]
    return jnp.cos(ang), jnp.sin(ang)


def _retention_layer(x, mod, g, w_in, w_out, cos, sin, tables, l, li, tm):
    B, S, Dm = x.shape
    H, dk, dv, C = RET_HEADS, RET_QK_DIM, RET_V_DIM, RET_KERNEL_CHUNK
    dmask, xi_b, zeta_b, g_chunk = tables
    row_spec = pl.BlockSpec((1, tm, Dm), lambda b, i: (b, i, 0))
    pos_spec = pl.BlockSpec((tm, dk // 2), lambda b, i: (i, 0))
    return pl.pallas_call(
        functools.partial(_ret_kernel, tm=tm),
        grid=(B, S // tm),
        in_specs=[
            row_spec,
            _mod_spec(mod, l),
            _resident_layer(g, l),
            _resident_layer(w_in, li),
            _resident_layer(w_out, li),
            pos_spec,
            pos_spec,
            _resident((H, C, C)),
            _resident((H, C, dv)),
            _resident((H, C, dv)),
            pl.BlockSpec(memory_space=pltpu.SMEM),
        ],
        out_specs=row_spec,
        out_shape=jax.ShapeDtypeStruct(x.shape, F32),
        scratch_shapes=[
            pltpu.VMEM((tm, w_in.shape[-1]), F32),
            pltpu.VMEM((tm, H * dv), BF16),
            pltpu.VMEM((H, dk, dv), F32),
        ],
        compiler_params=_params(),
        name="retention_sublayer",
    )(x, mod, g, w_in, w_out, cos, sin, dmask, xi_b, zeta_b, g_chunk)


def _kv_kernel(x_ref, mod_ref, g_ref, wk_ref, wvt_ref, k_ref, vt_ref):
    h = _adaln(x_ref[0], g_ref[...], mod_ref[0, 0:1, :], mod_ref[0, 1:2, :]).astype(BF16)
    k_ref[0] = _dot(h, wk_ref[...]).astype(BF16)
    vt_ref[0] = _dot_nt(wvt_ref[...], h).astype(BF16)


def _shared_kv(x, kv_mod, g, wk_pad, wv_t, tm):
    B, S, Dm = x.shape
    nk, nv = wk_pad.shape[1], wv_t.shape[0]
    return pl.pallas_call(
        _kv_kernel,
        grid=(B, S // tm),
        in_specs=[
            pl.BlockSpec((1, tm, Dm), lambda b, i: (b, i, 0)),
            pl.BlockSpec((1, 2, Dm), lambda b, i: (b, 0, 0)),
            _resident((1, Dm)),
            _resident(wk_pad.shape),
            _resident(wv_t.shape),
        ],
        out_specs=[pl.BlockSpec((1, tm, nk), lambda b, i: (b, i, 0)),
                   pl.BlockSpec((1, nv, tm), lambda b, i: (b, 0, i))],
        out_shape=[jax.ShapeDtypeStruct((B, S, nk), BF16),
                   jax.ShapeDtypeStruct((B, nv, S), BF16)],
        compiler_params=_params(),
        name="shared_kv",
    )(x, kv_mod, g, wk_pad, wv_t)


def _bias_kernel(table_ref, bucket_ref, win_ref, o_ref):
    hd = pl.program_id(0)
    bucket = bucket_ref[...]
    acc = jnp.zeros(bucket.shape, F32)
    for b in range(REL_BUCKETS):
        acc = jnp.where(bucket == b, table_ref[b, hd], acc)
    o_ref[0] = jnp.where(win_ref[...] != 0, acc, MASK_NEG)


def _bias_band_t(rel_bias):
    i = np.arange(BLOCK)[None, :]
    j = np.arange(2 * BLOCK)[:, None]
    dist = i + BLOCK - j
    n = np.maximum(dist, 0)
    max_exact = REL_BUCKETS // 2
    large = max_exact + (np.log(np.maximum(n, 1).astype(np.float32) / max_exact)
                         / math.log(REL_MAX_DIST / max_exact)
                         * (REL_BUCKETS - max_exact)).astype(np.int32)
    large = np.minimum(large, REL_BUCKETS - 1)
    bucket = np.where(n < max_exact, n, large).astype(np.int32)
    win = ((dist >= 0) & (dist < WINDOW)).astype(np.int32)
    full = pl.BlockSpec((2 * BLOCK, BLOCK), lambda h: (0, 0))
    return pl.pallas_call(
        _bias_kernel,
        grid=(SWA_HEADS,),
        in_specs=[pl.BlockSpec(memory_space=pltpu.SMEM), full, full],
        out_specs=pl.BlockSpec((1, 2 * BLOCK, BLOCK), lambda h: (h, 0, 0)),
        out_shape=jax.ShapeDtypeStruct((SWA_HEADS, 2 * BLOCK, BLOCK), F32),
        name="rel_bias_band",
    )(rel_bias, jnp.asarray(bucket), jnp.asarray(win))


def _swa_kernel(x_ref, mod_ref, g_ref, wqt_ref, wo_ref, kc_ref, kp_ref, vtc_ref, vtp_ref,
                bias_ref, sink_ref, o_ref, qt_ref, att_ref, *, tm, sub):
    KV, G, hd, KL = SWA_KV_HEADS, SWA_GROUP, SWA_HEAD_DIM, K_PAD_LANES
    n_q = SWA_HEADS * hd
    first_tile = pl.program_id(1) == 0
    shift, scale, gate_mod = mod_ref[0, 0:1, :], mod_ref[0, 1:2, :], mod_ref[0, 2:3, :]

    qt_ref[n_q:n_q + hd, :] = jnp.zeros((hd, tm), BF16)
    key_row = lax.broadcasted_iota(jnp.int32, (2 * BLOCK, GQ_ROWS), 0)
    edge_mask = jnp.where(jnp.logical_and(first_tile, key_row < BLOCK), MASK_NEG, 0.0)

    def scores(j, kh):
        cols = slice(j * BLOCK, (j + 1) * BLOCK)
        klanes = slice(kh * KL, (kh + 1) * KL)
        if j == 0:
            keys = jnp.concatenate([kp_ref[0, :, klanes], kc_ref[0, cols, klanes]], axis=0)
        else:
            keys = kc_ref[0, (j - 1) * BLOCK:(j + 1) * BLOCK, klanes]
        q_cat = jnp.concatenate(
            [qt_ref[(kh * G + g) * hd:(kh * G + g) * hd + KL, cols] for g in range(G)],
            axis=1)
        s = _dot(keys, q_cat) + bias_ref[kh]
        return s + edge_mask if j == 0 else s

    def weighted_values(j, kh, p):
        cols = slice(j * BLOCK, (j + 1) * BLOCK)
        vrows = slice(kh * hd, (kh + 1) * hd)
        if j == 0:
            vals_t = jnp.concatenate([vtp_ref[0, vrows, :], vtc_ref[0, vrows, cols]], axis=1)
        else:
            vals_t = vtc_ref[0, vrows, (j - 1) * BLOCK:(j + 1) * BLOCK]
        o_t = _dot(vals_t, p)
        for g in range(G):
            att_ref[(kh * G + g) * hd:(kh * G + g + 1) * hd, cols] = (
                o_t[:, g * BLOCK:(g + 1) * BLOCK].astype(BF16))

    def slabs_of(s):
        nb = sub // BLOCK
        return [(j, kh) for j in range(s * nb, (s + 1) * nb) for kh in range(KV)]

    def softmax_values(s, sc):
        slabs = slabs_of(s)
        m = [jnp.maximum(jnp.max(si, axis=0, keepdims=True), sink_ref[kh])
             for si, (_, kh) in zip(sc, slabs)]
        e = [jnp.exp(si - mi) for si, mi in zip(sc, m)]
        inv = [1.0 / (jnp.sum(ei, axis=0, keepdims=True) + jnp.exp(sink_ref[kh] - mi))
               for ei, mi, (_, kh) in zip(e, m, slabs)]
        for ei, ii, (j, kh) in zip(e, inv, slabs):
            weighted_values(j, kh, (ei * ii).astype(BF16))

    def project_q(rows):
        h = _adaln(x_ref[0, rows, :], g_ref[0:1, :], shift, scale).astype(BF16)
        qt_ref[0:n_q, rows] = (_dot_nt(wqt_ref[...], h) * (hd ** -0.5)).astype(BF16)

    def project_o(rows):
        y = _dot_tn(att_ref[:, rows], wo_ref[...])
        o_ref[0, rows, :] = (x_ref[0, rows, :]
                             + (1.0 + gate_mod) * _rmsnorm(y, g_ref[1:2, :]))

    subs = [slice(s * sub, (s + 1) * sub) for s in range(tm // sub)]
    project_q(subs[0])
    for s in range(len(subs)):
        sc = [scores(j, kh) for j, kh in slabs_of(s)]
        if s + 1 < len(subs):
            project_q(subs[s + 1])
        if s > 0:
            project_o(subs[s - 1])
        softmax_values(s, sc)
    project_o(subs[-1])


def _swa_layer(x, mod, g, w_q_t, w_o, k_pad, v_t, bias_t, sink_rows, l, li, tm, sub):
    B, S, Dm = x.shape
    nk, nv = k_pad.shape[-1], v_t.shape[1]
    n_q = w_q_t.shape[1]
    nb = tm // BLOCK
    row_spec = pl.BlockSpec((1, tm, Dm), lambda b, i: (b, i, 0))
    prev_blk = lambda i: jnp.maximum(i * nb - 1, 0)
    return pl.pallas_call(
        functools.partial(_swa_kernel, tm=tm, sub=sub),
        grid=(B, S // tm),
        in_specs=[
            row_spec,
            _mod_spec(mod, l),
            _resident_layer(g, l),
            _resident_layer(w_q_t, li),
            _resident_layer(w_o, li),
            pl.BlockSpec((1, tm, nk), lambda b, i: (b, i, 0)),
            pl.BlockSpec((1, BLOCK, nk), lambda b, i: (b, prev_blk(i), 0)),
            pl.BlockSpec((1, nv, tm), lambda b, i: (b, 0, i)),
            pl.BlockSpec((1, nv, BLOCK), lambda b, i: (b, 0, prev_blk(i))),
            _resident(bias_t.shape),
            _resident_layer(sink_rows, li),
        ],
        out_specs=row_spec,
        out_shape=jax.ShapeDtypeStruct(x.shape, F32),
        scratch_shapes=[
            pltpu.VMEM((n_q + SWA_HEAD_DIM, tm), BF16),
            pltpu.VMEM((n_q, tm), BF16),
        ],
        compiler_params=_params(),
        name="swa_sublayer",
    )(x, mod, g, w_q_t, w_o, k_pad, k_pad, v_t, v_t, bias_t, sink_rows)


def _kv_weights(kv_w):
    Dm = kv_w.shape[0]
    n = SWA_KV_HEADS * SWA_HEAD_DIM
    wk = kv_w[:, :n].reshape(Dm, SWA_KV_HEADS, SWA_HEAD_DIM)
    wk = jnp.pad(wk, ((0, 0), (0, 0), (0, K_PAD_LANES - SWA_HEAD_DIM)))
    return wk.reshape(Dm, SWA_KV_HEADS * K_PAD_LANES).astype(BF16), kv_w[:, n:].T.astype(BF16)


def kernel(x, c, norm_g, ada_w, ada_b, ret_w_in, ret_w_out, kv_norm_g, kv_ada_w, kv_ada_b, kv_w,
           swa_w_q, swa_w_o, swa_sinks, rel_bias, ffn_w_up, ffn_conv_w, ffn_conv_b, ffn_w_down):
    B, S, Dm = x.shape
    TM_FFN, SUB_FFN, TM_RET, TM_SWA, SUB_SWA, TM_KV = 512, 256, 512, 1024, 512, 512

    c_pad = jnp.zeros((V7X_SUBLANES, Dm), F32).at[:B].set(c)
    mod = _modulation(c_pad, ada_w, ada_b[:, None, :], 2048)
    mod = mod[:, :B].reshape(DEPTH, B, 6, Dm)
    kv_mod = _modulation(c_pad, kv_ada_w[None], kv_ada_b[None, None, :], 2048)
    kv_mod = kv_mod[0, :B].reshape(B, 2, Dm)

    cos, sin = _rotary_tables(S)
    tables = _retention_tables()
    bias_t = _bias_band_t(rel_bias).reshape(SWA_KV_HEADS, SWA_GROUP, 2 * BLOCK, BLOCK)
    bias_t = bias_t.transpose(0, 2, 1, 3).reshape(SWA_KV_HEADS, 2 * BLOCK, GQ_ROWS)
    sink_rows = jnp.repeat(swa_sinks.astype(F32), BLOCK, axis=1).reshape(
        swa_sinks.shape[0], SWA_KV_HEADS, 1, GQ_ROWS)

    ret_w_in, ret_w_out = ret_w_in.astype(BF16), ret_w_out.astype(BF16)
    swa_w_q_t, swa_w_o = swa_w_q.transpose(0, 2, 1).astype(BF16), swa_w_o.astype(BF16)
    ffn_w_up, ffn_w_down = ffn_w_up.astype(BF16), ffn_w_down.astype(BF16)
    ffn_conv_b = ffn_conv_b[:, None, :]

    k_pad = v_t = None
    for l in range(DEPTH):
        if l < N_A_LAYERS:
            x = _retention_layer(x, mod, norm_g, ret_w_in, ret_w_out, cos, sin, tables, l, l, TM_RET)
        else:
            x = _swa_layer(x, mod, norm_g, swa_w_q_t, swa_w_o, k_pad, v_t, bias_t, sink_rows,
                           l, l - N_A_LAYERS, TM_SWA, SUB_SWA)
        x = _ffn_layer(x, mod, norm_g, ffn_w_up, ffn_conv_w, ffn_conv_b, ffn_w_down,
                       l, TM_FFN, SUB_FFN)
        if l == N_A_LAYERS - 1:
            wk_pad, wv_t = _kv_weights(kv_w)
            k_pad, v_t = _shared_kv(x, kv_mod, kv_norm_g[None, :], wk_pad, wv_t, TM_KV)
    return x
```

```python
import functools
import math

import jax
import jax.numpy as jnp
import numpy as np
from jax import lax
from jax.experimental import pallas as pl
from jax.experimental.pallas import tpu as pltpu

D_MODEL = 1024
DEPTH = 4
N_A_LAYERS = DEPTH // 2
RET_HEADS = 4
RET_QK_DIM = D_MODEL // RET_HEADS
RET_V_DIM = 2 * D_MODEL // RET_HEADS
RET_CHUNK = 128
ROPE_BASE = 10000.0
SWA_HEADS = 16
SWA_KV_HEADS = 4
SWA_GROUP = SWA_HEADS // SWA_KV_HEADS
SWA_HEAD_DIM = 64
WINDOW = 128
BLOCK = WINDOW
REL_BUCKETS = 32
REL_MAX_DIST = 128
D_FF = 2816
CONV_WIDTH = 3
NORM_EPS = 1e-6

V7X_SUBLANES = 8
V7X_LANES = 128
V7X_MXU_DIM = 256
V7X_VMEM_LIMIT_BYTES = 58 * 1024 * 1024

MASK_NEG = -1e30
BF16 = jnp.bfloat16
F32 = jnp.float32

RET_KERNEL_CHUNK = V7X_MXU_DIM
RET_NORM_ROWS = 32
FFN_COL_CHUNK = V7X_MXU_DIM
FFN_UP_STAGE_ROWS, FFN_DOWN_STAGE_ROWS = 128, 256
RET_IN_STAGE_ROWS, RET_OUT_STAGE_ROWS = 64, 128
GQ_ROWS = SWA_GROUP * BLOCK
K_PAD_LANES = V7X_LANES

def _dot(a, b):
    return jnp.dot(a, b, preferred_element_type=F32)


def _dot_nt(a, b):
    return lax.dot_general(a, b, (((1,), (1,)), ((), ())), preferred_element_type=F32)


def _dot_tn(a, b):
    return lax.dot_general(a, b, (((0,), (0,)), ((), ())), preferred_element_type=F32)


def _sigmoid(x):
    return 1.0 / (1.0 + jnp.exp(-x))


def _adaln(x, g, shift, scale):
    ms = jnp.mean(x * x, axis=-1, keepdims=True)
    y = x * lax.rsqrt(ms + NORM_EPS) * g
    return y * (1.0 + scale) + shift


def _rmsnorm(y, g):
    ms = jnp.mean(y * y, axis=-1, keepdims=True)
    return y * lax.rsqrt(ms + NORM_EPS) * g


def _resident(shape):
    zeros = (0,) * len(shape)
    return pl.BlockSpec(shape, lambda b, i: zeros, pipeline_mode=pl.Buffered(1))


def _load_weight_bf16(w_hbm, w_bf16, stage, sem, rows):
    n_chunks = w_bf16.shape[0] // rows

    def chunk_copy(k):
        return pltpu.make_async_copy(
            w_hbm.at[pl.ds(k * rows, rows), :], stage.at[k % 2], sem.at[k % 2])

    chunk_copy(0).start()
    for k in range(n_chunks):
        if k + 1 < n_chunks:
            chunk_copy(k + 1).start()
        chunk_copy(k).wait()
        w_bf16[k * rows:(k + 1) * rows, :] = stage[k % 2].astype(BF16)


def _resident_layer(stacked, l):
    tail = (0,) * (stacked.ndim - 1)
    return pl.BlockSpec((None,) + stacked.shape[1:], lambda b, i: (l,) + tail,
                        pipeline_mode=pl.Buffered(1))


def _mod_spec(mod, l):
    return pl.BlockSpec((None, 1) + mod.shape[2:], lambda b, i: (l, b, 0, 0))


def _params():
    return pltpu.CompilerParams(
        dimension_semantics=("arbitrary", "arbitrary"),
        vmem_limit_bytes=V7X_VMEM_LIMIT_BYTES)


def _mod_kernel(c_ref, w_ref, b_ref, o_ref):
    c = c_ref[...]
    ca = (c * _sigmoid(c)).astype(BF16)
    o_ref[0] = _dot(ca, w_ref[0].astype(BF16)) + b_ref[0]


def _modulation(c_pad, w, b, tn):
    L, Dm, N = w.shape
    return pl.pallas_call(
        _mod_kernel,
        grid=(L, N // tn),
        in_specs=[
            pl.BlockSpec((V7X_SUBLANES, Dm), lambda l, j: (0, 0)),
            pl.BlockSpec((1, Dm, tn), lambda l, j: (l, 0, j)),
            pl.BlockSpec((1, 1, tn), lambda l, j: (l, 0, j)),
        ],
        out_specs=pl.BlockSpec((1, V7X_SUBLANES, tn), lambda l, j: (l, 0, j)),
        out_shape=jax.ShapeDtypeStruct((L, V7X_SUBLANES, N), F32),
        compiler_params=_params(),
        name="adaln_modulation",
    )(c_pad, w, b)


def _ffn_kernel(x_ref, mod_ref, g_ref, wup_hbm, cw_ref, cb_ref, wdn_hbm, o_ref,
                u_ref, act_ref, carry_ref, wup_ref, wdn_ref, up_stage, dn_stage, up_sem, dn_sem,
                *, tm, sub, layer):
    F = D_FF
    FC = FFN_COL_CHUNK

    @pl.when(jnp.logical_and(pl.program_id(0) == 0, pl.program_id(1) == 0))
    def _():
        _load_weight_bf16(wup_hbm.at[layer], wup_ref, up_stage, up_sem, up_stage.shape[1])
        _load_weight_bf16(wdn_hbm.at[layer], wdn_ref, dn_stage, dn_sem, dn_stage.shape[1])

    @pl.when(pl.program_id(1) == 0)
    def _():
        carry_ref[...] = jnp.zeros_like(carry_ref)

    shift, scale, gate_mod = mod_ref[0, 3:4, :], mod_ref[0, 4:5, :], mod_ref[0, 5:6, :]
    row = lax.broadcasted_iota(jnp.int32, (sub, FC), 0)

    def conv(slot, cols):
        u = u_ref[slot, :, cols]
        prev1 = carry_ref[V7X_SUBLANES - 1:V7X_SUBLANES, cols]
        prev2 = carry_ref[V7X_SUBLANES - 2:V7X_SUBLANES - 1, cols]
        u1 = jnp.where(row == 0, prev1, pltpu.roll(u, 1, 0))
        u2 = jnp.where(row == 0, prev2, jnp.where(row == 1, prev1, pltpu.roll(u, 2, 0)))
        return (cb_ref[0:1, cols] + cw_ref[0:1, cols] * u2 + cw_ref[1:2, cols] * u1
                + cw_ref[2:3, cols] * u)

    def up(s):
        rows = slice(s * sub, (s + 1) * sub)
        h = _adaln(x_ref[0, rows, :], g_ref[2:3, :], shift, scale).astype(BF16)
        u_ref[s % 2] = _dot(h, wup_ref[...])

    def activate(s):
        slot = s % 2
        for c in range(F // FC):
            gate = conv(slot, slice(c * FC, (c + 1) * FC))
            val = conv(slot, slice(F + c * FC, F + (c + 1) * FC))
            act_ref[slot, :, c * FC:(c + 1) * FC] = (gate * _sigmoid(gate) * val).astype(BF16)
        carry_ref[...] = u_ref[slot, sub - V7X_SUBLANES:sub, :]

    def down(s):
        rows = slice(s * sub, (s + 1) * sub)
        y = _dot(act_ref[s % 2], wdn_ref[...])
        o_ref[0, rows, :] = (x_ref[0, rows, :]
                             + (1.0 + gate_mod) * _rmsnorm(y, g_ref[3:4, :]))

    n_sub = tm // sub
    up(0)
    for s in range(n_sub):
        if s + 1 < n_sub:
            up(s + 1)
        activate(s)
        down(s)


def _ffn_layer(x, mod, g, w_up, conv_w, conv_b, w_down, l, tm, sub):
    B, S, Dm = x.shape
    F2 = w_up.shape[-1]
    row_spec = pl.BlockSpec((1, tm, Dm), lambda b, i: (b, i, 0))
    return pl.pallas_call(
        functools.partial(_ffn_kernel, tm=tm, sub=sub, layer=l),
        grid=(B, S // tm),
        in_specs=[
            row_spec,
            _mod_spec(mod, l),
            _resident_layer(g, l),
            pl.BlockSpec(memory_space=pl.ANY),
            _resident_layer(conv_w, l),
            _resident_layer(conv_b, l),
            pl.BlockSpec(memory_space=pl.ANY),
        ],
        out_specs=row_spec,
        out_shape=jax.ShapeDtypeStruct(x.shape, F32),
        scratch_shapes=[
            pltpu.VMEM((2, sub, F2), F32),
            pltpu.VMEM((2, sub, F2 // 2), BF16),
            pltpu.VMEM((V7X_SUBLANES, F2), F32),
            pltpu.VMEM((Dm, F2), BF16),
            pltpu.VMEM((F2 // 2, Dm), BF16),
            pltpu.VMEM((2, FFN_UP_STAGE_ROWS, F2), F32),
            pltpu.VMEM((2, FFN_DOWN_STAGE_ROWS, Dm), F32),
            pltpu.SemaphoreType.DMA((2,)),
            pltpu.SemaphoreType.DMA((2,)),
        ],
        compiler_params=_params(),
        name="conv_ffn_sublayer",
    )(x, mod, g, w_up, conv_w, conv_b, w_down)


def _ret_kernel(x_ref, mod_ref, g_ref, win_hbm, wout_hbm, cos_ref, sin_ref, dmask_ref,
                xi_ref, zeta_ref, gch_ref, o_ref, proj_ref, go_ref, state_ref,
                win_ref, wout_ref, in_stage, out_stage, in_sem, out_sem, *, tm, layer):
    H, dk, dv, C = RET_HEADS, RET_QK_DIM, RET_V_DIM, RET_KERNEL_CHUNK
    half = dk // 2
    k_off, v_off, g_off = H * dk, 2 * H * dk, 2 * H * dk + H * dv

    @pl.when(jnp.logical_and(pl.program_id(0) == 0, pl.program_id(1) == 0))
    def _():
        _load_weight_bf16(win_hbm.at[layer], win_ref, in_stage, in_sem, in_stage.shape[1])
        _load_weight_bf16(wout_hbm.at[layer], wout_ref, out_stage, out_sem, out_stage.shape[1])

    @pl.when(pl.program_id(1) == 0)
    def _():
        state_ref[...] = jnp.zeros_like(state_ref)

    shift, scale, gate_mod = mod_ref[0, 0:1, :], mod_ref[0, 1:2, :], mod_ref[0, 2:3, :]

    def rotary(t, cos, sin):
        t1, t2 = t[:, :half], t[:, half:]
        return jnp.concatenate([t1 * cos - t2 * sin, t2 * cos + t1 * sin], axis=-1)

    def project(rows):
        h = _adaln(x_ref[0, rows, :], g_ref[0:1, :], shift, scale).astype(BF16)
        proj_ref[rows, :] = _dot(h, win_ref[...])

    heads = range(H)

    def core(rows):
        cos, sin = cos_ref[rows, :], sin_ref[rows, :]
        q = [rotary(proj_ref[rows, hd * dk:(hd + 1) * dk], cos, sin).astype(BF16) for hd in heads]
        k = [(rotary(proj_ref[rows, k_off + hd * dk:k_off + (hd + 1) * dk], cos, sin)
              * (dk ** -0.5)).astype(BF16) for hd in heads]
        v = [proj_ref[rows, v_off + hd * dv:v_off + (hd + 1) * dv] for hd in heads]
        s = [_dot_nt(q[hd], k[hd]) for hd in heads]
        s = [(s[hd] * dmask_ref[hd]).astype(BF16) for hd in heads]
        inner = [_dot(s[hd], v[hd].astype(BF16)) for hd in heads]
        state = [state_ref[hd] for hd in heads]
        cross = [_dot(q[hd], state[hd].astype(BF16)) for hd in heads]
        kv = [_dot_tn(k[hd], (v[hd] * zeta_ref[hd]).astype(BF16)) for hd in heads]
        for hd in heads:
            state_ref[hd] = state[hd] * gch_ref[hd] + kv[hd]
        for hd in heads:
            for r0 in range(0, C, RET_NORM_ROWS):
                rs = slice(r0, r0 + RET_NORM_ROWS)
                out_rows = slice(rows.start + r0, rows.start + r0 + RET_NORM_ROWS)
                o = inner[hd][rs] + cross[hd][rs] * xi_ref[hd, rs, :]
                mu = jnp.mean(o, axis=-1, keepdims=True)
                oc = o - mu
                var = jnp.mean(oc * oc, axis=-1, keepdims=True)
                on = oc * lax.rsqrt(var + NORM_EPS)
                gate = proj_ref[out_rows, g_off + hd * dv:g_off + (hd + 1) * dv]
                go_ref[out_rows, hd * dv:(hd + 1) * dv] = (
                    gate * _sigmoid(gate) * on).astype(BF16)

    def finish(rows):
        y = _dot(go_ref[rows, :], wout_ref[...])
        o_ref[0, rows, :] = (x_ref[0, rows, :]
                             + (1.0 + gate_mod) * _rmsnorm(y, g_ref[1:2, :]))

    chunks = [slice(c * C, (c + 1) * C) for c in range(tm // C)]
    for rows in chunks:
        project(rows)
    for rows in chunks:
        core(rows)
    for rows in chunks:
        finish(rows)


def _retention_tables():
    H, C, dv = RET_HEADS, RET_KERNEL_CHUNK, RET_V_DIM
    log_gamma = jnp.log(1.0 - 2.0 ** (-5.0 - jnp.arange(H, dtype=F32)))
    idx = jnp.arange(C, dtype=F32)
    diff = idx[:, None] - idx[None, :]
    dmask = jnp.where(diff[None] >= 0,
                      jnp.exp(jnp.maximum(diff, 0.0)[None] * log_gamma[:, None, None]), 0.0)
    xi = jnp.exp((idx[None, :] + 1.0) * log_gamma[:, None])
    zeta = jnp.exp((C - 1.0 - idx[None, :]) * log_gamma[:, None])
    g_chunk = jnp.exp(C * log_gamma)
    xi_b = jnp.broadcast_to(xi[:, :, None], (H, C, dv))
    zeta_b = jnp.broadcast_to(zeta[:, :, None], (H, C, dv))
    return dmask, xi_b, zeta_b, g_chunk


def _rotary_tables(S):
    half = RET_QK_DIM // 2
    inv = ROPE_BASE ** (-jnp.arange(half, dtype=F32) / half)
    ang = jnp.arange(S).astype(F32)[:, None] * inv[None, :]
    return jnp.cos(ang), jnp.sin(ang)


def _retention_layer(x, mod, g, w_in, w_out, cos, sin, tables, l, li, tm):
    B, S, Dm = x.shape
    H, dk, dv, C = RET_HEADS, RET_QK_DIM, RET_V_DIM, RET_KERNEL_CHUNK
    dmask, xi_b, zeta_b, g_chunk = tables
    row_spec = pl.BlockSpec((1, tm, Dm), lambda b, i: (b, i, 0))
    pos_spec = pl.BlockSpec((tm, dk // 2), lambda b, i: (i, 0))
    return pl.pallas_call(
        functools.partial(_ret_kernel, tm=tm, layer=li),
        grid=(B, S // tm),
        in_specs=[
            row_spec,
            _mod_spec(mod, l),
            _resident_layer(g, l),
            pl.BlockSpec(memory_space=pl.ANY),
            pl.BlockSpec(memory_space=pl.ANY),
            pos_spec,
            pos_spec,
            _resident((H, C, C)),
            _resident((H, C, dv)),
            _resident((H, C, dv)),
            pl.BlockSpec(memory_space=pltpu.SMEM),
        ],
        out_specs=row_spec,
        out_shape=jax.ShapeDtypeStruct(x.shape, F32),
        scratch_shapes=[
            pltpu.VMEM((tm, w_in.shape[-1]), F32),
            pltpu.VMEM((tm, H * dv), BF16),
            pltpu.VMEM((H, dk, dv), F32),
            pltpu.VMEM(w_in.shape[1:], BF16),
            pltpu.VMEM(w_out.shape[1:], BF16),
            pltpu.VMEM((2, RET_IN_STAGE_ROWS, w_in.shape[-1]), F32),
            pltpu.VMEM((2, RET_OUT_STAGE_ROWS, w_out.shape[-1]), F32),
            pltpu.SemaphoreType.DMA((2,)),
            pltpu.SemaphoreType.DMA((2,)),
        ],
        compiler_params=_params(),
        name="retention_sublayer",
    )(x, mod, g, w_in, w_out, cos, sin, dmask, xi_b, zeta_b, g_chunk)


def _kv_kernel(x_ref, mod_ref, g_ref, wk_ref, wvt_ref, k_ref, vt_ref):
    h = _adaln(x_ref[0], g_ref[...], mod_ref[0, 0:1, :], mod_ref[0, 1:2, :]).astype(BF16)
    k_ref[0] = _dot(h, wk_ref[...]).astype(BF16)
    vt_ref[0] = _dot_nt(wvt_ref[...], h).astype(BF16)


def _shared_kv(x, kv_mod, g, wk_pad, wv_t, tm):
    B, S, Dm = x.shape
    nk, nv = wk_pad.shape[1], wv_t.shape[0]
    return pl.pallas_call(
        _kv_kernel,
        grid=(B, S // tm),
        in_specs=[
            pl.BlockSpec((1, tm, Dm), lambda b, i: (b, i, 0)),
            pl.BlockSpec((1, 2, Dm), lambda b, i: (b, 0, 0)),
            _resident((1, Dm)),
            _resident(wk_pad.shape),
            _resident(wv_t.shape),
        ],
        out_specs=[pl.BlockSpec((1, tm, nk), lambda b, i: (b, i, 0)),
                   pl.BlockSpec((1, nv, tm), lambda b, i: (b, 0, i))],
        out_shape=[jax.ShapeDtypeStruct((B, S, nk), BF16),
                   jax.ShapeDtypeStruct((B, nv, S), BF16)],
        compiler_params=_params(),
        name="shared_kv",
    )(x, kv_mod, g, wk_pad, wv_t)


def _bias_kernel(table_ref, bucket_ref, win_ref, o_ref):
    hd = pl.program_id(0)
    bucket = bucket_ref[...]
    acc = jnp.zeros(bucket.shape, F32)
    for b in range(REL_BUCKETS):
        acc = jnp.where(bucket == b, table_ref[b, hd], acc)
    o_ref[0] = jnp.where(win_ref[...] != 0, acc, MASK_NEG)


def _bias_band_t(rel_bias):
    i = np.arange(BLOCK)[None, :]
    j = np.arange(2 * BLOCK)[:, None]
    dist = i + BLOCK - j
    n = np.maximum(dist, 0)
    max_exact = REL_BUCKETS // 2
    large = max_exact + (np.log(np.maximum(n, 1).astype(np.float32) / max_exact)
                         / math.log(REL_MAX_DIST / max_exact)
                         * (REL_BUCKETS - max_exact)).astype(np.int32)
    large = np.minimum(large, REL_BUCKETS - 1)
    bucket = np.where(n < max_exact, n, large).astype(np.int32)
    win = ((dist >= 0) & (dist < WINDOW)).astype(np.int32)
    full = pl.BlockSpec((2 * BLOCK, BLOCK), lambda h: (0, 0))
    return pl.pallas_call(
        _bias_kernel,
        grid=(SWA_HEADS,),
        in_specs=[pl.BlockSpec(memory_space=pltpu.SMEM), full, full],
        out_specs=pl.BlockSpec((1, 2 * BLOCK, BLOCK), lambda h: (h, 0, 0)),
        out_shape=jax.ShapeDtypeStruct((SWA_HEADS, 2 * BLOCK, BLOCK), F32),
        name="rel_bias_band",
    )(rel_bias, jnp.asarray(bucket), jnp.asarray(win))


def _swa_kernel(x_ref, mod_ref, g_ref, wqt_ref, wo_ref, kc_ref, kp_ref, vtc_ref, vtp_ref,
                bias_ref, sink_ref, o_ref, qt_ref, att_ref, *, tm, sub):
    KV, G, hd, KL = SWA_KV_HEADS, SWA_GROUP, SWA_HEAD_DIM, K_PAD_LANES
    n_q = SWA_HEADS * hd
    first_tile = pl.program_id(1) == 0
    shift, scale, gate_mod = mod_ref[0, 0:1, :], mod_ref[0, 1:2, :], mod_ref[0, 2:3, :]

    qt_ref[n_q:n_q + hd, :] = jnp.zeros((hd, tm), BF16)
    key_row = lax.broadcasted_iota(jnp.int32, (2 * BLOCK, GQ_ROWS), 0)
    edge_mask = jnp.where(jnp.logical_and(first_tile, key_row < BLOCK), MASK_NEG, 0.0)

    def scores(j, kh):
        cols = slice(j * BLOCK, (j + 1) * BLOCK)
        klanes = slice(kh * KL, (kh + 1) * KL)
        if j == 0:
            keys = jnp.concatenate([kp_ref[0, :, klanes], kc_ref[0, cols, klanes]], axis=0)
        else:
            keys = kc_ref[0, (j - 1) * BLOCK:(j + 1) * BLOCK, klanes]
        q_cat = jnp.concatenate(
            [qt_ref[(kh * G + g) * hd:(kh * G + g) * hd + KL, cols] for g in range(G)],
            axis=1)
        s = _dot(keys, q_cat) + bias_ref[kh]
        return s + edge_mask if j == 0 else s

    def weighted_values(j, kh, p):
        cols = slice(j * BLOCK, (j + 1) * BLOCK)
        vrows = slice(kh * hd, (kh + 1) * hd)
        if j == 0:
            vals_t = jnp.concatenate([vtp_ref[0, vrows, :], vtc_ref[0, vrows, cols]], axis=1)
        else:
            vals_t = vtc_ref[0, vrows, (j - 1) * BLOCK:(j + 1) * BLOCK]
        o_t = _dot(vals_t, p)
        for g in range(G):
            att_ref[(kh * G + g) * hd:(kh * G + g + 1) * hd, cols] = (
                o_t[:, g * BLOCK:(g + 1) * BLOCK].astype(BF16))

    def slabs_of(s):
        nb = sub // BLOCK
        return [(j, kh) for j in range(s * nb, (s + 1) * nb) for kh in range(KV)]

    def softmax_values(s, sc):
        slabs = slabs_of(s)
        m = [jnp.maximum(jnp.max(si, axis=0, keepdims=True), sink_ref[kh])
             for si, (_, kh) in zip(sc, slabs)]
        e = [jnp.exp(si - mi) for si, mi in zip(sc, m)]
        inv = [1.0 / (jnp.sum(ei, axis=0, keepdims=True) + jnp.exp(sink_ref[kh] - mi))
               for ei, mi, (_, kh) in zip(e, m, slabs)]
        for ei, ii, (j, kh) in zip(e, inv, slabs):
            weighted_values(j, kh, (ei * ii).astype(BF16))

    def project_q(rows):
        h = _adaln(x_ref[0, rows, :], g_ref[0:1, :], shift, scale).astype(BF16)
        qt_ref[0:n_q, rows] = (_dot_nt(wqt_ref[...], h) * (hd ** -0.5)).astype(BF16)

    def project_o(rows):
        y = _dot_tn(att_ref[:, rows], wo_ref[...])
        o_ref[0, rows, :] = (x_ref[0, rows, :]
                             + (1.0 + gate_mod) * _rmsnorm(y, g_ref[1:2, :]))

    subs = [slice(s * sub, (s + 1) * sub) for s in range(tm // sub)]
    project_q(subs[0])
    for s in range(len(subs)):
        sc = [scores(j, kh) for j, kh in slabs_of(s)]
        if s + 1 < len(subs):
            project_q(subs[s + 1])
        if s > 0:
            project_o(subs[s - 1])
        softmax_values(s, sc)
    project_o(subs[-1])


def _swa_layer(x, mod, g, w_q_t, w_o, k_pad, v_t, bias_t, sink_rows, l, li, tm, sub):
    B, S, Dm = x.shape
    nk, nv = k_pad.shape[-1], v_t.shape[1]
    n_q = w_q_t.shape[1]
    nb = tm // BLOCK
    row_spec = pl.BlockSpec((1, tm, Dm), lambda b, i: (b, i, 0))
    prev_blk = lambda i: jnp.maximum(i * nb - 1, 0)
    return pl.pallas_call(
        functools.partial(_swa_kernel, tm=tm, sub=sub),
        grid=(B, S // tm),
        in_specs=[
            row_spec,
            _mod_spec(mod, l),
            _resident_layer(g, l),
            _resident_layer(w_q_t, li),
            _resident_layer(w_o, li),
            pl.BlockSpec((1, tm, nk), lambda b, i: (b, i, 0)),
            pl.BlockSpec((1, BLOCK, nk), lambda b, i: (b, prev_blk(i), 0)),
            pl.BlockSpec((1, nv, tm), lambda b, i: (b, 0, i)),
            pl.BlockSpec((1, nv, BLOCK), lambda b, i: (b, 0, prev_blk(i))),
            _resident(bias_t.shape),
            _resident_layer(sink_rows, li),
        ],
        out_specs=row_spec,
        out_shape=jax.ShapeDtypeStruct(x.shape, F32),
        scratch_shapes=[
            pltpu.VMEM((n_q + SWA_HEAD_DIM, tm), BF16),
            pltpu.VMEM((n_q, tm), BF16),
        ],
        compiler_params=_params(),
        name="swa_sublayer",
    )(x, mod, g, w_q_t, w_o, k_pad, k_pad, v_t, v_t, bias_t, sink_rows)


def _kv_weights(kv_w):
    Dm = kv_w.shape[0]
    n = SWA_KV_HEADS * SWA_HEAD_DIM
    wk = kv_w[:, :n].reshape(Dm, SWA_KV_HEADS, SWA_HEAD_DIM)
    wk = jnp.pad(wk, ((0, 0), (0, 0), (0, K_PAD_LANES - SWA_HEAD_DIM)))
    return wk.reshape(Dm, SWA_KV_HEADS * K_PAD_LANES).astype(BF16), kv_w[:, n:].T.astype(BF16)


def kernel(x, c, norm_g, ada_w, ada_b, ret_w_in, ret_w_out, kv_norm_g, kv_ada_w, kv_ada_b, kv_w,
           swa_w_q, swa_w_o, swa_sinks, rel_bias, ffn_w_up, ffn_conv_w, ffn_conv_b, ffn_w_down):
    B, S, Dm = x.shape
    TM_FFN, SUB_FFN, TM_RET, TM_SWA, SUB_SWA, TM_KV = 512, 256, 512, 1024, 512, 512

    c_pad = jnp.zeros((V7X_SUBLANES, Dm), F32).at[:B].set(c)
    mod = _modulation(c_pad, ada_w, ada_b[:, None, :], 2048)
    mod = mod[:, :B].reshape(DEPTH, B, 6, Dm)
    kv_mod = _modulation(c_pad, kv_ada_w[None], kv_ada_b[None, None, :], 2048)
    kv_mod = kv_mod[0, :B].reshape(B, 2, Dm)

    cos, sin = _rotary_tables(S)
    tables = _retention_tables()
    bias_t = _bias_band_t(rel_bias).reshape(SWA_KV_HEADS, SWA_GROUP, 2 * BLOCK, BLOCK)
    bias_t = bias_t.transpose(0, 2, 1, 3).reshape(SWA_KV_HEADS, 2 * BLOCK, GQ_ROWS)
    sink_rows = jnp.repeat(swa_sinks.astype(F32), BLOCK, axis=1).reshape(
        swa_sinks.shape[0], SWA_KV_HEADS, 1, GQ_ROWS)

    swa_w_q_t, swa_w_o = swa_w_q.transpose(0, 2, 1).astype(BF16), swa_w_o.astype(BF16)
    ffn_conv_b = ffn_conv_b[:, None, :]

    k_pad = v_t = None
    for l in range(DEPTH):
        if l < N_A_LAYERS:
            x = _retention_layer(x, mod, norm_g, ret_w_in, ret_w_out, cos, sin, tables, l, l, TM_RET)
        else:
            x = _swa_layer(x, mod, norm_g, swa_w_q_t, swa_w_o, k_pad, v_t, bias_t, sink_rows,
                           l, l - N_A_LAYERS, TM_SWA, SUB_SWA)
        x = _ffn_layer(x, mod, norm_g, ffn_w_up, ffn_conv_w, ffn_conv_b, ffn_w_down,
                       l, TM_FFN, SUB_FFN)
        if l == N_A_LAYERS - 1:
            wk_pad, wv_t = _kv_weights(kv_w)
            k_pad, v_t = _shared_kv(x, kv_mod, kv_norm_g[None, :], wk_pad, wv_t, TM_KV)
    return x
```

```python
import functools
import math

import jax
import jax.numpy as jnp
import numpy as np
from jax import lax
from jax.experimental import pallas as pl
from jax.experimental.pallas import tpu as pltpu

D_MODEL = 1024
DEPTH = 4
N_A_LAYERS = DEPTH // 2
RET_HEADS = 4
RET_QK_DIM = D_MODEL // RET_HEADS
RET_V_DIM = 2 * D_MODEL // RET_HEADS
RET_CHUNK = 128
ROPE_BASE = 10000.0
SWA_HEADS = 16
SWA_KV_HEADS = 4
SWA_GROUP = SWA_HEADS // SWA_KV_HEADS
SWA_HEAD_DIM = 64
WINDOW = 128
BLOCK = WINDOW
REL_BUCKETS = 32
REL_MAX_DIST = 128
D_FF = 2816
CONV_WIDTH = 3
NORM_EPS = 1e-6

V7X_SUBLANES = 8
V7X_LANES = 128
V7X_MXU_DIM = 256
V7X_VMEM_LIMIT_BYTES = 58 * 1024 * 1024

MASK_NEG = -1e30
BF16 = jnp.bfloat16
F32 = jnp.float32

RET_KERNEL_CHUNK = V7X_MXU_DIM
RET_NORM_ROWS = 32
FFN_COL_CHUNK = V7X_MXU_DIM
FFN_UP_STAGE_ROWS, FFN_DOWN_STAGE_ROWS = 128, 704
RET_IN_STAGE_ROWS, RET_OUT_STAGE_ROWS = 64, 256
ROT_SPLIT = 128
GQ_ROWS = SWA_GROUP * BLOCK
K_PAD_LANES = V7X_LANES

def _dot(a, b):
    return jnp.dot(a, b, preferred_element_type=F32)


def _dot_nt(a, b):
    return lax.dot_general(a, b, (((1,), (1,)), ((), ())), preferred_element_type=F32)


def _dot_tn(a, b):
    return lax.dot_general(a, b, (((0,), (0,)), ((), ())), preferred_element_type=F32)


def _sigmoid(x):
    return 1.0 / (1.0 + jnp.exp(-x))


def _adaln(x, g, shift, scale):
    ms = jnp.mean(x * x, axis=-1, keepdims=True)
    return x * lax.rsqrt(ms + NORM_EPS) * (g * (1.0 + scale)) + shift


def _gated_rmsnorm(y, g, gate):
    ms = jnp.mean(y * y, axis=-1, keepdims=True)
    return y * lax.rsqrt(ms + NORM_EPS) * (g * (1.0 + gate))


def _resident(shape):
    zeros = (0,) * len(shape)
    return pl.BlockSpec(shape, lambda b, i: zeros, pipeline_mode=pl.Buffered(1))


def _load_weight_bf16(w_hbm, w_bf16, stage, sem, rows):
    n_chunks = w_bf16.shape[0] // rows

    def chunk_copy(k):
        return pltpu.make_async_copy(
            w_hbm.at[pl.ds(k * rows, rows), :], stage.at[k % 2], sem.at[k % 2])

    chunk_copy(0).start()
    for k in range(n_chunks):
        if k + 1 < n_chunks:
            chunk_copy(k + 1).start()
        chunk_copy(k).wait()
        w_bf16[k * rows:(k + 1) * rows, :] = stage[k % 2].astype(BF16)


def _resident_layer(stacked, l):
    tail = (0,) * (stacked.ndim - 1)
    return pl.BlockSpec((None,) + stacked.shape[1:], lambda b, i: (l,) + tail,
                        pipeline_mode=pl.Buffered(1))


def _mod_spec(mod, l):
    return pl.BlockSpec((None, 1) + mod.shape[2:], lambda b, i: (l, b, 0, 0))


def _params():
    return pltpu.CompilerParams(
        dimension_semantics=("arbitrary", "arbitrary"),
        vmem_limit_bytes=V7X_VMEM_LIMIT_BYTES)


def _mod_kernel(c_ref, w_ref, b_ref, o_ref):
    c = c_ref[...]
    ca = (c * _sigmoid(c)).astype(BF16)
    o_ref[0] = _dot(ca, w_ref[0].astype(BF16)) + b_ref[0]


def _modulation(c_pad, w, b, tn):
    L, Dm, N = w.shape
    return pl.pallas_call(
        _mod_kernel,
        grid=(L, N // tn),
        in_specs=[
            pl.BlockSpec((V7X_SUBLANES, Dm), lambda l, j: (0, 0)),
            pl.BlockSpec((1, Dm, tn), lambda l, j: (l, 0, j)),
            pl.BlockSpec((1, 1, tn), lambda l, j: (l, 0, j)),
        ],
        out_specs=pl.BlockSpec((1, V7X_SUBLANES, tn), lambda l, j: (l, 0, j)),
        out_shape=jax.ShapeDtypeStruct((L, V7X_SUBLANES, N), F32),
        compiler_params=_params(),
        name="adaln_modulation",
    )(c_pad, w, b)


def _ffn_kernel(x_ref, mod_ref, g_ref, wup_hbm, cw_ref, cb_ref, wdn_hbm, o_ref,
                u_ref, act_ref, carry_ref, wup_ref, wdn_ref, up_stage, dn_stage, up_sem, dn_sem,
                *, tm, sub, layer):
    F = D_FF
    FC = FFN_COL_CHUNK

    @pl.when(jnp.logical_and(pl.program_id(0) == 0, pl.program_id(1) == 0))
    def _():
        _load_weight_bf16(wup_hbm.at[layer], wup_ref, up_stage, up_sem, up_stage.shape[1])
        _load_weight_bf16(wdn_hbm.at[layer], wdn_ref, dn_stage, dn_sem, dn_stage.shape[1])

    @pl.when(pl.program_id(1) == 0)
    def _():
        carry_ref[...] = jnp.zeros_like(carry_ref)

    shift, scale, gate_mod = mod_ref[0, 3:4, :], mod_ref[0, 4:5, :], mod_ref[0, 5:6, :]
    row = lax.broadcasted_iota(jnp.int32, (sub, FC), 0)

    def conv(slot, cols):
        u = u_ref[slot, :, cols]
        prev1 = carry_ref[V7X_SUBLANES - 1:V7X_SUBLANES, cols]
        prev2 = carry_ref[V7X_SUBLANES - 2:V7X_SUBLANES - 1, cols]
        u1 = jnp.where(row == 0, prev1, pltpu.roll(u, 1, 0))
        u2 = jnp.where(row == 0, prev2, jnp.where(row == 1, prev1, pltpu.roll(u, 2, 0)))
        return (cb_ref[0:1, cols] + cw_ref[0:1, cols] * u2 + cw_ref[1:2, cols] * u1
                + cw_ref[2:3, cols] * u)

    def up(s):
        rows = slice(s * sub, (s + 1) * sub)
        h = _adaln(x_ref[0, rows, :], g_ref[2:3, :], shift, scale).astype(BF16)
        u_ref[s % 2] = _dot(h, wup_ref[...])

    def activate(s):
        slot = s % 2
        for c in range(F // FC):
            gate = conv(slot, slice(c * FC, (c + 1) * FC))
            val = conv(slot, slice(F + c * FC, F + (c + 1) * FC))
            act_ref[slot, :, c * FC:(c + 1) * FC] = (gate * _sigmoid(gate) * val).astype(BF16)
        carry_ref[...] = u_ref[slot, sub - V7X_SUBLANES:sub, :]

    def down(s):
        rows = slice(s * sub, (s + 1) * sub)
        y = _dot(act_ref[s % 2], wdn_ref[...])
        o_ref[0, rows, :] = x_ref[0, rows, :] + _gated_rmsnorm(y, g_ref[3:4, :], gate_mod)

    n_sub = tm // sub
    up(0)
    for s in range(n_sub):
        if s + 1 < n_sub:
            up(s + 1)
        activate(s)
        down(s)


def _ffn_layer(x, mod, g, w_up, conv_w, conv_b, w_down, l, tm, sub):
    B, S, Dm = x.shape
    F2 = w_up.shape[-1]
    row_spec = pl.BlockSpec((1, tm, Dm), lambda b, i: (b, i, 0))
    return pl.pallas_call(
        functools.partial(_ffn_kernel, tm=tm, sub=sub, layer=l),
        grid=(B, S // tm),
        in_specs=[
            row_spec,
            _mod_spec(mod, l),
            _resident_layer(g, l),
            pl.BlockSpec(memory_space=pl.ANY),
            _resident_layer(conv_w, l),
            _resident_layer(conv_b, l),
            pl.BlockSpec(memory_space=pl.ANY),
        ],
        out_specs=row_spec,
        out_shape=jax.ShapeDtypeStruct(x.shape, F32),
        scratch_shapes=[
            pltpu.VMEM((2, sub, F2), F32),
            pltpu.VMEM((2, sub, F2 // 2), BF16),
            pltpu.VMEM((V7X_SUBLANES, F2), F32),
            pltpu.VMEM((Dm, F2), BF16),
            pltpu.VMEM((F2 // 2, Dm), BF16),
            pltpu.VMEM((2, FFN_UP_STAGE_ROWS, F2), F32),
            pltpu.VMEM((2, FFN_DOWN_STAGE_ROWS, Dm), F32),
            pltpu.SemaphoreType.DMA((2,)),
            pltpu.SemaphoreType.DMA((2,)),
        ],
        compiler_params=_params(),
        name="conv_ffn_sublayer",
    )(x, mod, g, w_up, conv_w, conv_b, w_down)


def _ret_kernel(x_ref, mod_ref, g_ref, win_hbm, wout_hbm, cos_ref, sin_ref, dmask_ref,
                xi_ref, zeta_ref, gch_ref, o_ref, proj_ref, go_ref, state_ref,
                win_ref, wout_ref, in_stage, out_stage, in_sem, out_sem, *, tm, layer):
    H, dk, dv, C = RET_HEADS, RET_QK_DIM, RET_V_DIM, RET_KERNEL_CHUNK
    half = dk // 2
    k_off, v_off, g_off = H * dk, 2 * H * dk, 2 * H * dk + H * dv

    @pl.when(jnp.logical_and(pl.program_id(0) == 0, pl.program_id(1) == 0))
    def _():
        _load_weight_bf16(win_hbm.at[layer], win_ref, in_stage, in_sem, in_stage.shape[1])
        _load_weight_bf16(wout_hbm.at[layer], wout_ref, out_stage, out_sem, out_stage.shape[1])

    @pl.when(pl.program_id(1) == 0)
    def _():
        state_ref[...] = jnp.zeros_like(state_ref)

    shift, scale, gate_mod = mod_ref[0, 0:1, :], mod_ref[0, 1:2, :], mod_ref[0, 2:3, :]

    def rotary(t, cos, sin):
        t1, t2 = t[:, :half], t[:, half:]
        return jnp.concatenate([t1 * cos - t2 * sin, t2 * cos + t1 * sin], axis=-1)

    def project(rows):
        h = _adaln(x_ref[0, rows, :], g_ref[0:1, :], shift, scale).astype(BF16)
        proj_ref[rows, :] = _dot(h, win_ref[...])

    heads = range(H)

    def core(rows):
        cos, sin = cos_ref[rows, :], sin_ref[rows, :]
        q = [rotary(proj_ref[rows, hd * dk:(hd + 1) * dk], cos, sin).astype(BF16) for hd in heads]
        k = [(rotary(proj_ref[rows, k_off + hd * dk:k_off + (hd + 1) * dk], cos, sin)
              * (dk ** -0.5)).astype(BF16) for hd in heads]
        v = [proj_ref[rows, v_off + hd * dv:v_off + (hd + 1) * dv] for hd in heads]
        s = [_dot_nt(q[hd], k[hd]) for hd in heads]
        s = [(s[hd] * dmask_ref[hd]).astype(BF16) for hd in heads]
        inner = [_dot(s[hd], v[hd].astype(BF16)) for hd in heads]
        state = [state_ref[hd] for hd in heads]
        cross = [_dot(q[hd], state[hd].astype(BF16)) for hd in heads]
        kv = [_dot_tn(k[hd], (v[hd] * zeta_ref[hd]).astype(BF16)) for hd in heads]
        for hd in heads:
            state_ref[hd] = state[hd] * gch_ref[hd] + kv[hd]
        for hd in heads:
            for r0 in range(0, C, RET_NORM_ROWS):
                rs = slice(r0, r0 + RET_NORM_ROWS)
                out_rows = slice(rows.start + r0, rows.start + r0 + RET_NORM_ROWS)
                o = inner[hd][rs] + cross[hd][rs] * xi_ref[hd, rs, :]
                mu = jnp.mean(o, axis=-1, keepdims=True)
                oc = o - mu
                var = jnp.mean(oc * oc, axis=-1, keepdims=True)
                on = oc * lax.rsqrt(var + NORM_EPS)
                gate = proj_ref[out_rows, g_off + hd * dv:g_off + (hd + 1) * dv]
                go_ref[out_rows, hd * dv:(hd + 1) * dv] = (
                    gate * _sigmoid(gate) * on).astype(BF16)

    def finish(rows):
        y = _dot(go_ref[rows, :], wout_ref[...])
        o_ref[0, rows, :] = x_ref[0, rows, :] + _gated_rmsnorm(y, g_ref[1:2, :], gate_mod)

    chunks = [slice(c * C, (c + 1) * C) for c in range(tm // C)]
    for rows in chunks:
        project(rows)
    for rows in chunks:
        core(rows)
    for rows in chunks:
        finish(rows)


def _retention_tables():
    H, C, dv = RET_HEADS, RET_KERNEL_CHUNK, RET_V_DIM
    log_gamma = jnp.log(1.0 - 2.0 ** (-5.0 - jnp.arange(H, dtype=F32)))
    idx = jnp.arange(C, dtype=F32)
    diff = idx[:, None] - idx[None, :]
    dmask = jnp.where(diff[None] >= 0,
                      jnp.exp(jnp.maximum(diff, 0.0)[None] * log_gamma[:, None, None]), 0.0)
    xi = jnp.exp((idx[None, :] + 1.0) * log_gamma[:, None])
    zeta = jnp.exp((C - 1.0 - idx[None, :]) * log_gamma[:, None])
    g_chunk = jnp.exp(C * log_gamma)
    xi_b = jnp.broadcast_to(xi[:, :, None], (H, C, dv))
    zeta_b = jnp.broadcast_to(zeta[:, :, None], (H, C, dv))
    return dmask, xi_b, zeta_b, g_chunk


def _rotary_tables(S):
    half = RET_QK_DIM // 2
    inv = ROPE_BASE ** (-jnp.arange(half, dtype=F32) / half)
    hi = (jnp.arange(S // ROT_SPLIT) * ROT_SPLIT).astype(F32)[:, None, None] * inv
    lo = jnp.arange(ROT_SPLIT).astype(F32)[None, :, None] * inv
    cos = jnp.cos(hi) * jnp.cos(lo) - jnp.sin(hi) * jnp.sin(lo)
    sin = jnp.sin(hi) * jnp.cos(lo) + jnp.cos(hi) * jnp.sin(lo)
    return cos.reshape(S, half), sin.reshape(S, half)


def _retention_layer(x, mod, g, w_in, w_out, cos, sin, tables, l, li, tm):
    B, S, Dm = x.shape
    H, dk, dv, C = RET_HEADS, RET_QK_DIM, RET_V_DIM, RET_KERNEL_CHUNK
    dmask, xi_b, zeta_b, g_chunk = tables
    row_spec = pl.BlockSpec((1, tm, Dm), lambda b, i: (b, i, 0))
    pos_spec = pl.BlockSpec((tm, dk // 2), lambda b, i: (i, 0))
    return pl.pallas_call(
        functools.partial(_ret_kernel, tm=tm, layer=li),
        grid=(B, S // tm),
        in_specs=[
            row_spec,
            _mod_spec(mod, l),
            _resident_layer(g, l),
            pl.BlockSpec(memory_space=pl.ANY),
            pl.BlockSpec(memory_space=pl.ANY),
            pos_spec,
            pos_spec,
            _resident((H, C, C)),
            _resident((H, C, dv)),
            _resident((H, C, dv)),
            pl.BlockSpec(memory_space=pltpu.SMEM),
        ],
        out_specs=row_spec,
        out_shape=jax.ShapeDtypeStruct(x.shape, F32),
        scratch_shapes=[
            pltpu.VMEM((tm, w_in.shape[-1]), F32),
            pltpu.VMEM((tm, H * dv), BF16),
            pltpu.VMEM((H, dk, dv), F32),
            pltpu.VMEM(w_in.shape[1:], BF16),
            pltpu.VMEM(w_out.shape[1:], BF16),
            pltpu.VMEM((2, RET_IN_STAGE_ROWS, w_in.shape[-1]), F32),
            pltpu.VMEM((2, RET_OUT_STAGE_ROWS, w_out.shape[-1]), F32),
            pltpu.SemaphoreType.DMA((2,)),
            pltpu.SemaphoreType.DMA((2,)),
        ],
        compiler_params=_params(),
        name="retention_sublayer",
    )(x, mod, g, w_in, w_out, cos, sin, dmask, xi_b, zeta_b, g_chunk)


def _kv_kernel(x_ref, mod_ref, g_ref, wk_ref, wvt_ref, k_ref, vt_ref):
    h = _adaln(x_ref[0], g_ref[...], mod_ref[0, 0:1, :], mod_ref[0, 1:2, :]).astype(BF16)
    k_ref[0] = _dot(h, wk_ref[...]).astype(BF16)
    vt_ref[0] = _dot_nt(wvt_ref[...], h).astype(BF16)


def _shared_kv(x, kv_mod, g, wk_pad, wv_t, tm):
    B, S, Dm = x.shape
    nk, nv = wk_pad.shape[1], wv_t.shape[0]
    return pl.pallas_call(
        _kv_kernel,
        grid=(B, S // tm),
        in_specs=[
            pl.BlockSpec((1, tm, Dm), lambda b, i: (b, i, 0)),
            pl.BlockSpec((1, 2, Dm), lambda b, i: (b, 0, 0)),
            _resident((1, Dm)),
            _resident(wk_pad.shape),
            _resident(wv_t.shape),
        ],
        out_specs=[pl.BlockSpec((1, tm, nk), lambda b, i: (b, i, 0)),
                   pl.BlockSpec((1, nv, tm), lambda b, i: (b, 0, i))],
        out_shape=[jax.ShapeDtypeStruct((B, S, nk), BF16),
                   jax.ShapeDtypeStruct((B, nv, S), BF16)],
        compiler_params=_params(),
        name="shared_kv",
    )(x, kv_mod, g, wk_pad, wv_t)


def _bias_kernel(table_ref, bucket_ref, win_ref, o_ref):
    hd = pl.program_id(0)
    bucket = bucket_ref[...]
    acc = jnp.zeros(bucket.shape, F32)
    for b in range(REL_BUCKETS):
        acc = jnp.where(bucket == b, table_ref[b, hd], acc)
    o_ref[0] = jnp.where(win_ref[...] != 0, acc, MASK_NEG)


def _bias_band_t(rel_bias):
    i = np.arange(BLOCK)[None, :]
    j = np.arange(2 * BLOCK)[:, None]
    dist = i + BLOCK - j
    n = np.maximum(dist, 0)
    max_exact = REL_BUCKETS // 2
    large = max_exact + (np.log(np.maximum(n, 1).astype(np.float32) / max_exact)
                         / math.log(REL_MAX_DIST / max_exact)
                         * (REL_BUCKETS - max_exact)).astype(np.int32)
    large = np.minimum(large, REL_BUCKETS - 1)
    bucket = np.where(n < max_exact, n, large).astype(np.int32)
    win = ((dist >= 0) & (dist < WINDOW)).astype(np.int32)
    full = pl.BlockSpec((2 * BLOCK, BLOCK), lambda h: (0, 0))
    return pl.pallas_call(
        _bias_kernel,
        grid=(SWA_HEADS,),
        in_specs=[pl.BlockSpec(memory_space=pltpu.SMEM), full, full],
        out_specs=pl.BlockSpec((1, 2 * BLOCK, BLOCK), lambda h: (h, 0, 0)),
        out_shape=jax.ShapeDtypeStruct((SWA_HEADS, 2 * BLOCK, BLOCK), F32),
        name="rel_bias_band",
    )(rel_bias, jnp.asarray(bucket), jnp.asarray(win))


def _swa_kernel(x_ref, mod_ref, g_ref, wqt_ref, wo_ref, kc_ref, kp_ref, vtc_ref, vtp_ref,
                bias_ref, sink_ref, o_ref, qt_ref, att_ref, *, tm, sub):
    KV, G, hd, KL = SWA_KV_HEADS, SWA_GROUP, SWA_HEAD_DIM, K_PAD_LANES
    n_q = SWA_HEADS * hd
    first_tile = pl.program_id(1) == 0
    shift, scale, gate_mod = mod_ref[0, 0:1, :], mod_ref[0, 1:2, :], mod_ref[0, 2:3, :]

    qt_ref[n_q:n_q + hd, :] = jnp.zeros((hd, tm), BF16)
    key_row = lax.broadcasted_iota(jnp.int32, (2 * BLOCK, GQ_ROWS), 0)
    edge_mask = jnp.where(jnp.logical_and(first_tile, key_row < BLOCK), MASK_NEG, 0.0)

    def scores(j, kh):
        cols = slice(j * BLOCK, (j + 1) * BLOCK)
        klanes = slice(kh * KL, (kh + 1) * KL)
        if j == 0:
            keys = jnp.concatenate([kp_ref[0, :, klanes], kc_ref[0, cols, klanes]], axis=0)
        else:
            keys = kc_ref[0, (j - 1) * BLOCK:(j + 1) * BLOCK, klanes]
        q_cat = jnp.concatenate(
            [qt_ref[(kh * G + g) * hd:(kh * G + g) * hd + KL, cols] for g in range(G)],
            axis=1)
        s = _dot(keys, q_cat) + bias_ref[kh]
        return s + edge_mask if j == 0 else s

    def weighted_values(j, kh, p):
        cols = slice(j * BLOCK, (j + 1) * BLOCK)
        vrows = slice(kh * hd, (kh + 1) * hd)
        if j == 0:
            vals_t = jnp.concatenate([vtp_ref[0, vrows, :], vtc_ref[0, vrows, cols]], axis=1)
        else:
            vals_t = vtc_ref[0, vrows, (j - 1) * BLOCK:(j + 1) * BLOCK]
        o_t = _dot(vals_t, p)
        for g in range(G):
            att_ref[(kh * G + g) * hd:(kh * G + g + 1) * hd, cols] = (
                o_t[:, g * BLOCK:(g + 1) * BLOCK].astype(BF16))

    def slabs_of(s):
        nb = sub // BLOCK
        return [(j, kh) for j in range(s * nb, (s + 1) * nb) for kh in range(KV)]

    def softmax_values(s, sc):
        slabs = slabs_of(s)
        m = [jnp.maximum(jnp.max(si, axis=0, keepdims=True), sink_ref[kh])
             for si, (_, kh) in zip(sc, slabs)]
        e = [jnp.exp(si - mi) for si, mi in zip(sc, m)]
        inv = [1.0 / (jnp.sum(ei, axis=0, keepdims=True) + jnp.exp(sink_ref[kh] - mi))
               for ei, mi, (_, kh) in zip(e, m, slabs)]
        for ei, ii, (j, kh) in zip(e, inv, slabs):
            weighted_values(j, kh, (ei * ii).astype(BF16))

    def project_q(rows):
        h = _adaln(x_ref[0, rows, :], g_ref[0:1, :], shift, scale).astype(BF16)
        qt_ref[0:n_q, rows] = (_dot_nt(wqt_ref[...], h) * (hd ** -0.5)).astype(BF16)

    def project_o(rows):
        y = _dot_tn(att_ref[:, rows], wo_ref[...])
        o_ref[0, rows, :] = x_ref[0, rows, :] + _gated_rmsnorm(y, g_ref[1:2, :], gate_mod)

    subs = [slice(s * sub, (s + 1) * sub) for s in range(tm // sub)]
    project_q(subs[0])
    for s in range(len(subs)):
        sc = [scores(j, kh) for j, kh in slabs_of(s)]
        if s + 1 < len(subs):
            project_q(subs[s + 1])
        if s > 0:
            project_o(subs[s - 1])
        softmax_values(s, sc)
    project_o(subs[-1])


def _swa_layer(x, mod, g, w_q_t, w_o, k_pad, v_t, bias_t, sink_rows, l, li, tm, sub):
    B, S, Dm = x.shape
    nk, nv = k_pad.shape[-1], v_t.shape[1]
    n_q = w_q_t.shape[1]
    nb = tm // BLOCK
    row_spec = pl.BlockSpec((1, tm, Dm), lambda b, i: (b, i, 0))
    prev_blk = lambda i: jnp.maximum(i * nb - 1, 0)
    return pl.pallas_call(
        functools.partial(_swa_kernel, tm=tm, sub=sub),
        grid=(B, S // tm),
        in_specs=[
            row_spec,
            _mod_spec(mod, l),
            _resident_layer(g, l),
            _resident_layer(w_q_t, li),
            _resident_layer(w_o, li),
            pl.BlockSpec((1, tm, nk), lambda b, i: (b, i, 0)),
            pl.BlockSpec((1, BLOCK, nk), lambda b, i: (b, prev_blk(i), 0)),
            pl.BlockSpec((1, nv, tm), lambda b, i: (b, 0, i)),
            pl.BlockSpec((1, nv, BLOCK), lambda b, i: (b, 0, prev_blk(i))),
            _resident(bias_t.shape),
            _resident_layer(sink_rows, li),
        ],
        out_specs=row_spec,
        out_shape=jax.ShapeDtypeStruct(x.shape, F32),
        scratch_shapes=[
            pltpu.VMEM((n_q + SWA_HEAD_DIM, tm), BF16),
            pltpu.VMEM((n_q, tm), BF16),
        ],
        compiler_params=_params(),
        name="swa_sublayer",
    )(x, mod, g, w_q_t, w_o, k_pad, k_pad, v_t, v_t, bias_t, sink_rows)


def _kv_weights(kv_w):
    Dm = kv_w.shape[0]
    n = SWA_KV_HEADS * SWA_HEAD_DIM
    wk = kv_w[:, :n].reshape(Dm, SWA_KV_HEADS, SWA_HEAD_DIM)
    wk = jnp.pad(wk, ((0, 0), (0, 0), (0, K_PAD_LANES - SWA_HEAD_DIM)))
    return wk.reshape(Dm, SWA_KV_HEADS * K_PAD_LANES).astype(BF16), kv_w[:, n:].T.astype(BF16)


def kernel(x, c, norm_g, ada_w, ada_b, ret_w_in, ret_w_out, kv_norm_g, kv_ada_w, kv_ada_b, kv_w,
           swa_w_q, swa_w_o, swa_sinks, rel_bias, ffn_w_up, ffn_conv_w, ffn_conv_b, ffn_w_down):
    B, S, Dm = x.shape
    TM_FFN, SUB_FFN, TM_RET, TM_SWA, SUB_SWA, TM_KV = 512, 256, 512, 1024, 512, 512

    c_pad = jnp.zeros((V7X_SUBLANES, Dm), F32).at[:B].set(c)
    mod = _modulation(c_pad, ada_w, ada_b[:, None, :], 2048)
    mod = mod[:, :B].reshape(DEPTH, B, 6, Dm)
    kv_mod = _modulation(c_pad, kv_ada_w[None], kv_ada_b[None, None, :], 2048)
    kv_mod = kv_mod[0, :B].reshape(B, 2, Dm)

    cos, sin = _rotary_tables(S)
    tables = _retention_tables()
    bias_t = _bias_band_t(rel_bias).reshape(SWA_KV_HEADS, SWA_GROUP, 2 * BLOCK, BLOCK)
    bias_t = bias_t.transpose(0, 2, 1, 3).reshape(SWA_KV_HEADS, 2 * BLOCK, GQ_ROWS)
    sink_rows = jnp.repeat(swa_sinks.astype(F32), BLOCK, axis=1).reshape(
        swa_sinks.shape[0], SWA_KV_HEADS, 1, GQ_ROWS)

    swa_w_q_t, swa_w_o = swa_w_q.transpose(0, 2, 1).astype(BF16), swa_w_o.astype(BF16)
    ffn_conv_b = ffn_conv_b[:, None, :]

    k_pad = v_t = None
    for l in range(DEPTH):
        if l < N_A_LAYERS:
            x = _retention_layer(x, mod, norm_g, ret_w_in, ret_w_out, cos, sin, tables, l, l, TM_RET)
        else:
            x = _swa_layer(x, mod, norm_g, swa_w_q_t, swa_w_o, k_pad, v_t, bias_t, sink_rows,
                           l, l - N_A_LAYERS, TM_SWA, SUB_SWA)
        x = _ffn_layer(x, mod, norm_g, ffn_w_up, ffn_conv_w, ffn_conv_b, ffn_w_down,
                       l, TM_FFN, SUB_FFN)
        if l == N_A_LAYERS - 1:
            wk_pad, wv_t = _kv_weights(kv_w)
            k_pad, v_t = _shared_kv(x, kv_mod, kv_norm_g[None, :], wk_pad, wv_t, TM_KV)
    return x
```

```python
import functools
import math
from typing import NamedTuple

import jax
import jax.numpy as jnp
import numpy as np
from jax import lax
from jax.experimental import pallas as pl
from jax.experimental.pallas import tpu as pltpu

D_MODEL = 1024
DEPTH = 4
N_A_LAYERS = DEPTH // 2
RET_HEADS = 4
RET_QK_DIM = D_MODEL // RET_HEADS
RET_V_DIM = 2 * D_MODEL // RET_HEADS
ROPE_BASE = 10000.0
SWA_HEADS = 16
SWA_KV_HEADS = 4
SWA_GROUP = SWA_HEADS // SWA_KV_HEADS
SWA_HEAD_DIM = 64
WINDOW = 128
BLOCK = WINDOW
REL_BUCKETS = 32
REL_MAX_DIST = 128
D_FF = 2816
NORM_EPS = 1e-6

V7X_SUBLANES = 8
V7X_LANES = 128
V7X_MXU_DIM = 256
V7X_VMEM_BYTES = 64 * 1024 * 1024
V7X_VMEM_LIMIT_BYTES = V7X_VMEM_BYTES - 6 * 1024 * 1024
MOD_COL_TILE = 2048

MASK_NEG = -1e30
BF16 = jnp.bfloat16
F32 = jnp.float32

RET_KERNEL_CHUNK = V7X_MXU_DIM
RET_NORM_ROWS = 32
FFN_COL_CHUNK = V7X_MXU_DIM
FFN_UP_STAGE_ROWS, FFN_DOWN_STAGE_ROWS = 128, 704
RET_IN_STAGE_ROWS, RET_OUT_STAGE_ROWS = 64, 256
ROT_SPLIT = 128
GQ_ROWS = SWA_GROUP * BLOCK
K_PAD_LANES = V7X_LANES


def _dot(a, b):
    return jnp.dot(a, b, preferred_element_type=F32)


def _dot_nt(a, b):
    return lax.dot_general(a, b, (((1,), (1,)), ((), ())), preferred_element_type=F32)


def _dot_tn(a, b):
    return lax.dot_general(a, b, (((0,), (0,)), ((), ())), preferred_element_type=F32)


def _sigmoid(x):
    return 1.0 / (1.0 + jnp.exp(-x))


def _adaln(x, g, shift, scale):
    ms = jnp.mean(x * x, axis=-1, keepdims=True)
    return x * lax.rsqrt(ms + NORM_EPS) * (g * (1.0 + scale)) + shift


def _gated_rmsnorm(y, g, gate):
    ms = jnp.mean(y * y, axis=-1, keepdims=True)
    return y * lax.rsqrt(ms + NORM_EPS) * (g * (1.0 + gate))


def _resident(shape):
    zeros = (0,) * len(shape)
    return pl.BlockSpec(shape, lambda b, i: zeros, pipeline_mode=pl.Buffered(1))


def _load_weight_bf16(w_hbm, w_bf16, stage, sem, rows):
    n_chunks = w_bf16.shape[0] // rows

    def chunk_copy(k):
        return pltpu.make_async_copy(
            w_hbm.at[pl.ds(k * rows, rows), :], stage.at[k % 2], sem.at[k % 2])

    chunk_copy(0).start()
    for k in range(n_chunks):
        if k + 1 < n_chunks:
            chunk_copy(k + 1).start()
        chunk_copy(k).wait()
        w_bf16[k * rows:(k + 1) * rows, :] = stage[k % 2].astype(BF16)


def _resident_layer(stacked, l):
    tail = (0,) * (stacked.ndim - 1)
    return pl.BlockSpec((None,) + stacked.shape[1:], lambda b, i: (l,) + tail,
                        pipeline_mode=pl.Buffered(1))


def _mod_spec(mod, l):
    return pl.BlockSpec((None, 1) + mod.shape[2:], lambda b, i: (l, b, 0, 0))


def _params():
    return pltpu.CompilerParams(
        dimension_semantics=("arbitrary", "arbitrary"),
        vmem_limit_bytes=V7X_VMEM_LIMIT_BYTES)


def _mod_kernel(c_ref, w_ref, b_ref, o_ref):
    c = c_ref[...]
    ca = (c * _sigmoid(c)).astype(BF16)
    o_ref[0] = _dot(ca, w_ref[0].astype(BF16)) + b_ref[0]


def _modulation(c_pad, w, b, tn):
    L, Dm, N = w.shape
    return pl.pallas_call(
        _mod_kernel,
        grid=(L, N // tn),
        in_specs=[
            pl.BlockSpec((V7X_SUBLANES, Dm), lambda l, j: (0, 0)),
            pl.BlockSpec((1, Dm, tn), lambda l, j: (l, 0, j)),
            pl.BlockSpec((1, 1, tn), lambda l, j: (l, 0, j)),
        ],
        out_specs=pl.BlockSpec((1, V7X_SUBLANES, tn), lambda l, j: (l, 0, j)),
        out_shape=jax.ShapeDtypeStruct((L, V7X_SUBLANES, N), F32),
        compiler_params=_params(),
        name="adaln_modulation",
    )(c_pad, w, b)


def _ffn_kernel(x_ref, mod_ref, g_ref, wup_hbm, cw_ref, cb_ref, wdn_hbm, o_ref,
                u_ref, act_ref, carry_ref, wup_ref, wdn_ref, up_stage, dn_stage, up_sem, dn_sem,
                *, tm, sub, layer):
    F = D_FF
    FC = FFN_COL_CHUNK

    @pl.when(jnp.logical_and(pl.program_id(0) == 0, pl.program_id(1) == 0))
    def _():
        _load_weight_bf16(wup_hbm.at[layer], wup_ref, up_stage, up_sem, up_stage.shape[1])
        _load_weight_bf16(wdn_hbm.at[layer], wdn_ref, dn_stage, dn_sem, dn_stage.shape[1])

    @pl.when(pl.program_id(1) == 0)
    def _():
        carry_ref[...] = jnp.zeros_like(carry_ref)

    shift, scale, gate_mod = mod_ref[0, 3:4, :], mod_ref[0, 4:5, :], mod_ref[0, 5:6, :]
    row = lax.broadcasted_iota(jnp.int32, (sub, FC), 0)

    def conv(slot, cols):
        u = u_ref[slot, :, cols]
        prev1 = carry_ref[V7X_SUBLANES - 1:V7X_SUBLANES, cols]
        prev2 = carry_ref[V7X_SUBLANES - 2:V7X_SUBLANES - 1, cols]
        u1 = jnp.where(row == 0, prev1, pltpu.roll(u, 1, 0))
        u2 = jnp.where(row == 0, prev2, jnp.where(row == 1, prev1, pltpu.roll(u, 2, 0)))
        return (cb_ref[0:1, cols] + cw_ref[0:1, cols] * u2 + cw_ref[1:2, cols] * u1
                + cw_ref[2:3, cols] * u)

    def up(s):
        rows = slice(s * sub, (s + 1) * sub)
        h = _adaln(x_ref[0, rows, :], g_ref[2:3, :], shift, scale).astype(BF16)
        u_ref[s % 2] = _dot(h, wup_ref[...])

    def activate(s):
        slot = s % 2
        for c in range(F // FC):
            gate = conv(slot, slice(c * FC, (c + 1) * FC))
            val = conv(slot, slice(F + c * FC, F + (c + 1) * FC))
            act_ref[slot, :, c * FC:(c + 1) * FC] = (gate * _sigmoid(gate) * val).astype(BF16)
        carry_ref[...] = u_ref[slot, sub - V7X_SUBLANES:sub, :]

    def down(s):
        rows = slice(s * sub, (s + 1) * sub)
        y = _dot(act_ref[s % 2], wdn_ref[...])
        o_ref[0, rows, :] = x_ref[0, rows, :] + _gated_rmsnorm(y, g_ref[3:4, :], gate_mod)

    n_sub = tm // sub
    up(0)
    for s in range(n_sub):
        if s + 1 < n_sub:
            up(s + 1)
        activate(s)
        down(s)


def _ffn_layer(x, mod, g, w_up, conv_w, conv_b, w_down, l, tm, sub):
    B, S, Dm = x.shape
    F2 = w_up.shape[-1]
    row_spec = pl.BlockSpec((1, tm, Dm), lambda b, i: (b, i, 0))
    return pl.pallas_call(
        functools.partial(_ffn_kernel, tm=tm, sub=sub, layer=l),
        grid=(B, S // tm),
        in_specs=[
            row_spec,
            _mod_spec(mod, l),
            _resident_layer(g, l),
            pl.BlockSpec(memory_space=pl.ANY),
            _resident_layer(conv_w, l),
            _resident_layer(conv_b, l),
            pl.BlockSpec(memory_space=pl.ANY),
        ],
        out_specs=row_spec,
        out_shape=jax.ShapeDtypeStruct(x.shape, F32),
        scratch_shapes=[
            pltpu.VMEM((2, sub, F2), F32),
            pltpu.VMEM((2, sub, F2 // 2), BF16),
            pltpu.VMEM((V7X_SUBLANES, F2), F32),
            pltpu.VMEM((Dm, F2), BF16),
            pltpu.VMEM((F2 // 2, Dm), BF16),
            pltpu.VMEM((2, FFN_UP_STAGE_ROWS, F2), F32),
            pltpu.VMEM((2, FFN_DOWN_STAGE_ROWS, Dm), F32),
            pltpu.SemaphoreType.DMA((2,)),
            pltpu.SemaphoreType.DMA((2,)),
        ],
        compiler_params=_params(),
        name="conv_ffn_sublayer",
    )(x, mod, g, w_up, conv_w, conv_b, w_down)


def _ret_kernel(x_ref, mod_ref, g_ref, win_hbm, wout_hbm, cos_ref, sin_ref, dmask_ref,
                xi_ref, zeta_ref, gch_ref, o_ref, proj_ref, go_ref, state_ref,
                win_ref, wout_ref, in_stage, out_stage, in_sem, out_sem, *, tm, layer):
    H, dk, dv, C = RET_HEADS, RET_QK_DIM, RET_V_DIM, RET_KERNEL_CHUNK
    half = dk // 2
    k_off, v_off, g_off = H * dk, 2 * H * dk, 2 * H * dk + H * dv

    @pl.when(jnp.logical_and(pl.program_id(0) == 0, pl.program_id(1) == 0))
    def _():
        _load_weight_bf16(win_hbm.at[layer], win_ref, in_stage, in_sem, in_stage.shape[1])
        _load_weight_bf16(wout_hbm.at[layer], wout_ref, out_stage, out_sem, out_stage.shape[1])

    @pl.when(pl.program_id(1) == 0)
    def _():
        state_ref[...] = jnp.zeros_like(state_ref)

    shift, scale, gate_mod = mod_ref[0, 0:1, :], mod_ref[0, 1:2, :], mod_ref[0, 2:3, :]

    def rotary(t, cos, sin):
        t1, t2 = t[:, :half], t[:, half:]
        return jnp.concatenate([t1 * cos - t2 * sin, t2 * cos + t1 * sin], axis=-1)

    def project(rows):
        h = _adaln(x_ref[0, rows, :], g_ref[0:1, :], shift, scale).astype(BF16)
        proj_ref[rows, :] = _dot(h, win_ref[...])

    heads = range(H)

    def core(rows):
        cos, sin = cos_ref[rows, :], sin_ref[rows, :]
        q = [rotary(proj_ref[rows, hd * dk:(hd + 1) * dk], cos, sin).astype(BF16) for hd in heads]
        k = [(rotary(proj_ref[rows, k_off + hd * dk:k_off + (hd + 1) * dk], cos, sin)
              * (dk ** -0.5)).astype(BF16) for hd in heads]
        v = [proj_ref[rows, v_off + hd * dv:v_off + (hd + 1) * dv] for hd in heads]
        s = [_dot_nt(q[hd], k[hd]) for hd in heads]
        s = [(s[hd] * dmask_ref[hd]).astype(BF16) for hd in heads]
        inner = [_dot(s[hd], v[hd].astype(BF16)) for hd in heads]
        state = [state_ref[hd] for hd in heads]
        cross = [_dot(q[hd], state[hd].astype(BF16)) for hd in heads]
        kv = [_dot_tn(k[hd], (v[hd] * zeta_ref[hd]).astype(BF16)) for hd in heads]
        for hd in heads:
            state_ref[hd] = state[hd] * gch_ref[hd] + kv[hd]
        for hd in heads:
            for r0 in range(0, C, RET_NORM_ROWS):
                rs = slice(r0, r0 + RET_NORM_ROWS)
                out_rows = slice(rows.start + r0, rows.start + r0 + RET_NORM_ROWS)
                o = inner[hd][rs] + cross[hd][rs] * xi_ref[hd, rs, :]
                mu = jnp.mean(o, axis=-1, keepdims=True)
                oc = o - mu
                var = jnp.mean(oc * oc, axis=-1, keepdims=True)
                on = oc * lax.rsqrt(var + NORM_EPS)
                gate = proj_ref[out_rows, g_off + hd * dv:g_off + (hd + 1) * dv]
                go_ref[out_rows, hd * dv:(hd + 1) * dv] = (
                    gate * _sigmoid(gate) * on).astype(BF16)

    def finish(rows):
        y = _dot(go_ref[rows, :], wout_ref[...])
        o_ref[0, rows, :] = x_ref[0, rows, :] + _gated_rmsnorm(y, g_ref[1:2, :], gate_mod)

    chunks = [slice(c * C, (c + 1) * C) for c in range(tm // C)]
    for rows in chunks:
        project(rows)
    for rows in chunks:
        core(rows)
    for rows in chunks:
        finish(rows)


def _retention_tables():
    H, C, dv = RET_HEADS, RET_KERNEL_CHUNK, RET_V_DIM
    log_gamma = jnp.log(1.0 - 2.0 ** (-5.0 - jnp.arange(H, dtype=F32)))
    idx = jnp.arange(C, dtype=F32)
    diff = idx[:, None] - idx[None, :]
    dmask = jnp.where(diff[None] >= 0,
                      jnp.exp(jnp.maximum(diff, 0.0)[None] * log_gamma[:, None, None]), 0.0)
    xi = jnp.exp((idx[None, :] + 1.0) * log_gamma[:, None])
    zeta = jnp.exp((C - 1.0 - idx[None, :]) * log_gamma[:, None])
    g_chunk = jnp.exp(C * log_gamma)
    xi_b = jnp.broadcast_to(xi[:, :, None], (H, C, dv))
    zeta_b = jnp.broadcast_to(zeta[:, :, None], (H, C, dv))
    return dmask, xi_b, zeta_b, g_chunk


def _rotary_tables(S):
    half = RET_QK_DIM // 2
    inv = ROPE_BASE ** (-jnp.arange(half, dtype=F32) / half)
    hi = (jnp.arange(S // ROT_SPLIT) * ROT_SPLIT).astype(F32)[:, None, None] * inv
    lo = jnp.arange(ROT_SPLIT).astype(F32)[None, :, None] * inv
    cos = jnp.cos(hi) * jnp.cos(lo) - jnp.sin(hi) * jnp.sin(lo)
    sin = jnp.sin(hi) * jnp.cos(lo) + jnp.cos(hi) * jnp.sin(lo)
    return cos.reshape(S, half), sin.reshape(S, half)


def _retention_layer(x, mod, g, w_in, w_out, cos, sin, tables, l, li, tm):
    B, S, Dm = x.shape
    H, dk, dv, C = RET_HEADS, RET_QK_DIM, RET_V_DIM, RET_KERNEL_CHUNK
    dmask, xi_b, zeta_b, g_chunk = tables
    row_spec = pl.BlockSpec((1, tm, Dm), lambda b, i: (b, i, 0))
    pos_spec = pl.BlockSpec((tm, dk // 2), lambda b, i: (i, 0))
    return pl.pallas_call(
        functools.partial(_ret_kernel, tm=tm, layer=li),
        grid=(B, S // tm),
        in_specs=[
            row_spec,
            _mod_spec(mod, l),
            _resident_layer(g, l),
            pl.BlockSpec(memory_space=pl.ANY),
            pl.BlockSpec(memory_space=pl.ANY),
            pos_spec,
            pos_spec,
            _resident((H, C, C)),
            _resident((H, C, dv)),
            _resident((H, C, dv)),
            pl.BlockSpec(memory_space=pltpu.SMEM),
        ],
        out_specs=row_spec,
        out_shape=jax.ShapeDtypeStruct(x.shape, F32),
        scratch_shapes=[
            pltpu.VMEM((tm, w_in.shape[-1]), F32),
            pltpu.VMEM((tm, H * dv), BF16),
            pltpu.VMEM((H, dk, dv), F32),
            pltpu.VMEM(w_in.shape[1:], BF16),
            pltpu.VMEM(w_out.shape[1:], BF16),
            pltpu.VMEM((2, RET_IN_STAGE_ROWS, w_in.shape[-1]), F32),
            pltpu.VMEM((2, RET_OUT_STAGE_ROWS, w_out.shape[-1]), F32),
            pltpu.SemaphoreType.DMA((2,)),
            pltpu.SemaphoreType.DMA((2,)),
        ],
        compiler_params=_params(),
        name="retention_sublayer",
    )(x, mod, g, w_in, w_out, cos, sin, dmask, xi_b, zeta_b, g_chunk)


def _kv_kernel(x_ref, mod_ref, g_ref, wk_ref, wvt_ref, k_ref, vt_ref):
    h = _adaln(x_ref[0], g_ref[...], mod_ref[0, 0:1, :], mod_ref[0, 1:2, :]).astype(BF16)
    k_ref[0] = _dot(h, wk_ref[...]).astype(BF16)
    vt_ref[0] = _dot_nt(wvt_ref[...], h).astype(BF16)


def _shared_kv(x, kv_mod, g, wk_pad, wv_t, tm):
    B, S, Dm = x.shape
    nk, nv = wk_pad.shape[1], wv_t.shape[0]
    return pl.pallas_call(
        _kv_kernel,
        grid=(B, S // tm),
        in_specs=[
            pl.BlockSpec((1, tm, Dm), lambda b, i: (b, i, 0)),
            pl.BlockSpec((1, 2, Dm), lambda b, i: (b, 0, 0)),
            _resident((1, Dm)),
            _resident(wk_pad.shape),
            _resident(wv_t.shape),
        ],
        out_specs=[pl.BlockSpec((1, tm, nk), lambda b, i: (b, i, 0)),
                   pl.BlockSpec((1, nv, tm), lambda b, i: (b, 0, i))],
        out_shape=[jax.ShapeDtypeStruct((B, S, nk), BF16),
                   jax.ShapeDtypeStruct((B, nv, S), BF16)],
        compiler_params=_params(),
        name="shared_kv",
    )(x, kv_mod, g, wk_pad, wv_t)


def _bias_kernel(table_ref, bucket_ref, win_ref, o_ref):
    hd = pl.program_id(0)
    bucket = bucket_ref[...]
    acc = jnp.zeros(bucket.shape, F32)
    for b in range(REL_BUCKETS):
        acc = jnp.where(bucket == b, table_ref[b, hd], acc)
    o_ref[0] = jnp.where(win_ref[...] != 0, acc, MASK_NEG)


def _bias_band_t(rel_bias):
    i = np.arange(BLOCK)[None, :]
    j = np.arange(2 * BLOCK)[:, None]
    dist = i + BLOCK - j
    n = np.maximum(dist, 0)
    max_exact = REL_BUCKETS // 2
    large = max_exact + (np.log(np.maximum(n, 1).astype(np.float32) / max_exact)
                         / math.log(REL_MAX_DIST / max_exact)
                         * (REL_BUCKETS - max_exact)).astype(np.int32)
    large = np.minimum(large, REL_BUCKETS - 1)
    bucket = np.where(n < max_exact, n, large).astype(np.int32)
    win = ((dist >= 0) & (dist < WINDOW)).astype(np.int32)
    full = pl.BlockSpec((2 * BLOCK, BLOCK), lambda h: (0, 0))
    return pl.pallas_call(
        _bias_kernel,
        grid=(SWA_HEADS,),
        in_specs=[pl.BlockSpec(memory_space=pltpu.SMEM), full, full],
        out_specs=pl.BlockSpec((1, 2 * BLOCK, BLOCK), lambda h: (h, 0, 0)),
        out_shape=jax.ShapeDtypeStruct((SWA_HEADS, 2 * BLOCK, BLOCK), F32),
        name="rel_bias_band",
    )(rel_bias, jnp.asarray(bucket), jnp.asarray(win))


def _swa_kernel(x_ref, mod_ref, g_ref, wqt_ref, wo_ref, kc_ref, kp_ref, vtc_ref, vtp_ref,
                bias_ref, sink_ref, o_ref, qt_ref, att_ref, *, tm, sub):
    KV, G, hd, KL = SWA_KV_HEADS, SWA_GROUP, SWA_HEAD_DIM, K_PAD_LANES
    n_q = SWA_HEADS * hd
    first_tile = pl.program_id(1) == 0
    shift, scale, gate_mod = mod_ref[0, 0:1, :], mod_ref[0, 1:2, :], mod_ref[0, 2:3, :]

    qt_ref[n_q:n_q + hd, :] = jnp.zeros((hd, tm), BF16)
    key_row = lax.broadcasted_iota(jnp.int32, (2 * BLOCK, GQ_ROWS), 0)
    edge_mask = jnp.where(jnp.logical_and(first_tile, key_row < BLOCK), MASK_NEG, 0.0)

    def scores(j, kh):
        cols = slice(j * BLOCK, (j + 1) * BLOCK)
        klanes = slice(kh * KL, (kh + 1) * KL)
        if j == 0:
            keys = jnp.concatenate([kp_ref[0, :, klanes], kc_ref[0, cols, klanes]], axis=0)
        else:
            keys = kc_ref[0, (j - 1) * BLOCK:(j + 1) * BLOCK, klanes]
        q_cat = jnp.concatenate(
            [qt_ref[(kh * G + g) * hd:(kh * G + g) * hd + KL, cols] for g in range(G)],
            axis=1)
        s = _dot(keys, q_cat) + bias_ref[kh]
        return s + edge_mask if j == 0 else s

    def weighted_values(j, kh, p):
        cols = slice(j * BLOCK, (j + 1) * BLOCK)
        vrows = slice(kh * hd, (kh + 1) * hd)
        if j == 0:
            vals_t = jnp.concatenate([vtp_ref[0, vrows, :], vtc_ref[0, vrows, cols]], axis=1)
        else:
            vals_t = vtc_ref[0, vrows, (j - 1) * BLOCK:(j + 1) * BLOCK]
        o_t = _dot(vals_t, p)
        for g in range(G):
            att_ref[(kh * G + g) * hd:(kh * G + g + 1) * hd, cols] = (
                o_t[:, g * BLOCK:(g + 1) * BLOCK].astype(BF16))

    def slabs_of(s):
        nb = sub // BLOCK
        return [(j, kh) for j in range(s * nb, (s + 1) * nb) for kh in range(KV)]

    def softmax_values(s, sc):
        slabs = slabs_of(s)
        m = [jnp.maximum(jnp.max(si, axis=0, keepdims=True), sink_ref[kh])
             for si, (_, kh) in zip(sc, slabs)]
        e = [jnp.exp(si - mi) for si, mi in zip(sc, m)]
        inv = [1.0 / (jnp.sum(ei, axis=0, keepdims=True) + jnp.exp(sink_ref[kh] - mi))
               for ei, mi, (_, kh) in zip(e, m, slabs)]
        for ei, ii, (j, kh) in zip(e, inv, slabs):
            weighted_values(j, kh, (ei * ii).astype(BF16))

    def project_q(rows):
        h = _adaln(x_ref[0, rows, :], g_ref[0:1, :], shift, scale).astype(BF16)
        qt_ref[0:n_q, rows] = (_dot_nt(wqt_ref[...], h) * (hd ** -0.5)).astype(BF16)

    def project_o(rows):
        y = _dot_tn(att_ref[:, rows], wo_ref[...])
        o_ref[0, rows, :] = x_ref[0, rows, :] + _gated_rmsnorm(y, g_ref[1:2, :], gate_mod)

    subs = [slice(s * sub, (s + 1) * sub) for s in range(tm // sub)]
    project_q(subs[0])
    for s in range(len(subs)):
        sc = [scores(j, kh) for j, kh in slabs_of(s)]
        if s + 1 < len(subs):
            project_q(subs[s + 1])
        if s > 0:
            project_o(subs[s - 1])
        softmax_values(s, sc)
    project_o(subs[-1])


def _swa_layer(x, mod, g, w_q_t, w_o, k_pad, v_t, bias_t, sink_rows, l, li, tm, sub):
    B, S, Dm = x.shape
    nk, nv = k_pad.shape[-1], v_t.shape[1]
    n_q = w_q_t.shape[1]
    nb = tm // BLOCK
    row_spec = pl.BlockSpec((1, tm, Dm), lambda b, i: (b, i, 0))
    prev_blk = lambda i: jnp.maximum(i * nb - 1, 0)
    return pl.pallas_call(
        functools.partial(_swa_kernel, tm=tm, sub=sub),
        grid=(B, S // tm),
        in_specs=[
            row_spec,
            _mod_spec(mod, l),
            _resident_layer(g, l),
            _resident_layer(w_q_t, li),
            _resident_layer(w_o, li),
            pl.BlockSpec((1, tm, nk), lambda b, i: (b, i, 0)),
            pl.BlockSpec((1, BLOCK, nk), lambda b, i: (b, prev_blk(i), 0)),
            pl.BlockSpec((1, nv, tm), lambda b, i: (b, 0, i)),
            pl.BlockSpec((1, nv, BLOCK), lambda b, i: (b, 0, prev_blk(i))),
            _resident(bias_t.shape),
            _resident_layer(sink_rows, li),
        ],
        out_specs=row_spec,
        out_shape=jax.ShapeDtypeStruct(x.shape, F32),
        scratch_shapes=[
            pltpu.VMEM((n_q + SWA_HEAD_DIM, tm), BF16),
            pltpu.VMEM((n_q, tm), BF16),
        ],
        compiler_params=_params(),
        name="swa_sublayer",
    )(x, mod, g, w_q_t, w_o, k_pad, k_pad, v_t, v_t, bias_t, sink_rows)


def _kv_weights(kv_w):
    Dm = kv_w.shape[0]
    n = SWA_KV_HEADS * SWA_HEAD_DIM
    wk = kv_w[:, :n].reshape(Dm, SWA_KV_HEADS, SWA_HEAD_DIM)
    wk = jnp.pad(wk, ((0, 0), (0, 0), (0, K_PAD_LANES - SWA_HEAD_DIM)))
    return wk.reshape(Dm, SWA_KV_HEADS * K_PAD_LANES).astype(BF16), kv_w[:, n:].T.astype(BF16)


class _Tiles(NamedTuple):
    ffn: int
    ffn_sub: int
    retention: int
    swa: int
    swa_sub: int
    kv: int


def _tile_plan(S):
    sub = V7X_MXU_DIM
    plan = _Tiles(ffn=2 * sub, ffn_sub=sub, retention=2 * RET_KERNEL_CHUNK,
                  swa=4 * sub, swa_sub=2 * sub, kv=4 * sub)
    assert all(S % t == 0 for t in plan), (S, plan)
    return plan


def kernel(x, c, norm_g, ada_w, ada_b, ret_w_in, ret_w_out, kv_norm_g, kv_ada_w, kv_ada_b, kv_w,
           swa_w_q, swa_w_o, swa_sinks, rel_bias, ffn_w_up, ffn_conv_w, ffn_conv_b, ffn_w_down):
    B, S, Dm = x.shape
    tiles = _tile_plan(S)

    c_pad = jnp.zeros((V7X_SUBLANES, Dm), F32).at[:B].set(c)
    mod = _modulation(c_pad, ada_w, ada_b[:, None, :], MOD_COL_TILE)
    mod = mod[:, :B].reshape(DEPTH, B, 6, Dm)
    kv_mod = _modulation(c_pad, kv_ada_w[None], kv_ada_b[None, None, :], MOD_COL_TILE)
    kv_mod = kv_mod[0, :B].reshape(B, 2, Dm)

    cos, sin = _rotary_tables(S)
    tables = _retention_tables()
    bias_t = _bias_band_t(rel_bias).reshape(SWA_KV_HEADS, SWA_GROUP, 2 * BLOCK, BLOCK)
    bias_t = bias_t.transpose(0, 2, 1, 3).reshape(SWA_KV_HEADS, 2 * BLOCK, GQ_ROWS)
    sink_rows = jnp.repeat(swa_sinks.astype(F32), BLOCK, axis=1).reshape(
        swa_sinks.shape[0], SWA_KV_HEADS, 1, GQ_ROWS)

    swa_w_q_t, swa_w_o = swa_w_q.transpose(0, 2, 1).astype(BF16), swa_w_o.astype(BF16)
    ffn_conv_b = ffn_conv_b[:, None, :]

    k_pad = v_t = None
    for l in range(DEPTH):
        if l < N_A_LAYERS:
            x = _retention_layer(x, mod, norm_g, ret_w_in, ret_w_out, cos, sin, tables, l, l,
                                 tiles.retention)
        else:
            x = _swa_layer(x, mod, norm_g, swa_w_q_t, swa_w_o, k_pad, v_t, bias_t, sink_rows,
                           l, l - N_A_LAYERS, tiles.swa, tiles.swa_sub)
        x = _ffn_layer(x, mod, norm_g, ffn_w_up, ffn_conv_w, ffn_conv_b, ffn_w_down,
                       l, tiles.ffn, tiles.ffn_sub)
        if l == N_A_LAYERS - 1:
            wk_pad, wv_t = _kv_weights(kv_w)
            k_pad, v_t = _shared_kv(x, kv_mod, kv_norm_g[None, :], wk_pad, wv_t, tiles.kv)
    return x
```

```python
import functools
import math
from typing import NamedTuple

import jax
import jax.numpy as jnp
import numpy as np
from jax import lax
from jax.experimental import pallas as pl
from jax.experimental.pallas import tpu as pltpu

D_MODEL = 1024
DEPTH = 4
N_A_LAYERS = DEPTH // 2
RET_HEADS = 4
RET_QK_DIM = D_MODEL // RET_HEADS
RET_V_DIM = 2 * D_MODEL // RET_HEADS
ROPE_BASE = 10000.0
SWA_HEADS = 16
SWA_KV_HEADS = 4
SWA_GROUP = SWA_HEADS // SWA_KV_HEADS
SWA_HEAD_DIM = 64
WINDOW = 128
BLOCK = WINDOW
REL_BUCKETS = 32
REL_MAX_DIST = 128
D_FF = 2816
NORM_EPS = 1e-6

V7X_SUBLANES = 8
V7X_LANES = 128
V7X_MXU_DIM = 256
V7X_VMEM_BYTES = 64 * 1024 * 1024
V7X_VMEM_LIMIT_BYTES = V7X_VMEM_BYTES - 6 * 1024 * 1024
MOD_COL_TILE = 2048

MASK_NEG = -1e30
BF16 = jnp.bfloat16
F32 = jnp.float32

RET_KERNEL_CHUNK = V7X_MXU_DIM
RET_NORM_ROWS = 32
FFN_COL_CHUNK = V7X_MXU_DIM
FFN_UP_STAGE_ROWS, FFN_DOWN_STAGE_ROWS = 128, 704
RET_IN_STAGE_ROWS, RET_OUT_STAGE_ROWS = 64, 256
ROT_SPLIT = 128
GQ_ROWS = SWA_GROUP * BLOCK
K_PAD_LANES = V7X_LANES


def _dot(a, b):
    return jnp.dot(a, b, preferred_element_type=F32)


def _dot_nt(a, b):
    return lax.dot_general(a, b, (((1,), (1,)), ((), ())), preferred_element_type=F32)


def _dot_tn(a, b):
    return lax.dot_general(a, b, (((0,), (0,)), ((), ())), preferred_element_type=F32)


def _sigmoid(x):
    return 1.0 / (1.0 + jnp.exp(-x))


def _adaln(x, g, shift, scale):
    ms = jnp.mean(x * x, axis=-1, keepdims=True)
    return x * lax.rsqrt(ms + NORM_EPS) * (g * (1.0 + scale)) + shift


def _gated_rmsnorm(y, g, gate):
    ms = jnp.mean(y * y, axis=-1, keepdims=True)
    return y * lax.rsqrt(ms + NORM_EPS) * (g * (1.0 + gate))


def _resident(shape):
    zeros = (0,) * len(shape)
    return pl.BlockSpec(shape, lambda b, i: zeros, pipeline_mode=pl.Buffered(1))


class _WeightLoad(NamedTuple):
    hbm: object
    bf16: object
    stage: object
    sem: object


def _load_weights_bf16(loads):
    def rows(ld):
        return ld.stage.shape[1]

    def n_chunks(ld):
        return ld.bf16.shape[0] // rows(ld)

    def chunk_copy(ld, k):
        return pltpu.make_async_copy(
            ld.hbm.at[pl.ds(k * rows(ld), rows(ld)), :], ld.stage.at[k % 2], ld.sem.at[k % 2])

    for ld in loads:
        chunk_copy(ld, 0).start()
    for k in range(max(n_chunks(ld) for ld in loads)):
        for ld in loads:
            if k < n_chunks(ld):
                if k + 1 < n_chunks(ld):
                    chunk_copy(ld, k + 1).start()
                chunk_copy(ld, k).wait()
                ld.bf16[k * rows(ld):(k + 1) * rows(ld), :] = ld.stage[k % 2].astype(BF16)


def _resident_layer(stacked, l):
    tail = (0,) * (stacked.ndim - 1)
    return pl.BlockSpec((None,) + stacked.shape[1:], lambda b, i: (l,) + tail,
                        pipeline_mode=pl.Buffered(1))


def _mod_spec(mod, l):
    return pl.BlockSpec((None, 1) + mod.shape[2:], lambda b, i: (l, b, 0, 0))


def _params():
    return pltpu.CompilerParams(
        dimension_semantics=("arbitrary", "arbitrary"),
        vmem_limit_bytes=V7X_VMEM_LIMIT_BYTES)


def _mod_kernel(c_ref, w_ref, b_ref, o_ref):
    c = c_ref[...]
    ca = (c * _sigmoid(c)).astype(BF16)
    o_ref[0] = _dot(ca, w_ref[0].astype(BF16)) + b_ref[0]


def _modulation(c_pad, w, b, tn):
    L, Dm, N = w.shape
    return pl.pallas_call(
        _mod_kernel,
        grid=(L, N // tn),
        in_specs=[
            pl.BlockSpec((V7X_SUBLANES, Dm), lambda l, j: (0, 0)),
            pl.BlockSpec((1, Dm, tn), lambda l, j: (l, 0, j)),
            pl.BlockSpec((1, 1, tn), lambda l, j: (l, 0, j)),
        ],
        out_specs=pl.BlockSpec((1, V7X_SUBLANES, tn), lambda l, j: (l, 0, j)),
        out_shape=jax.ShapeDtypeStruct((L, V7X_SUBLANES, N), F32),
        compiler_params=_params(),
        name="adaln_modulation",
    )(c_pad, w, b)


def _ffn_kernel(x_ref, mod_ref, g_ref, wup_hbm, cw_ref, cb_ref, wdn_hbm, o_ref,
                u_ref, act_ref, carry_ref, wup_ref, wdn_ref, up_stage, dn_stage, up_sem, dn_sem,
                *, tm, sub, layer):
    F = D_FF
    FC = FFN_COL_CHUNK

    @pl.when(jnp.logical_and(pl.program_id(0) == 0, pl.program_id(1) == 0))
    def _():
        _load_weights_bf16([_WeightLoad(wup_hbm.at[layer], wup_ref, up_stage, up_sem),
                            _WeightLoad(wdn_hbm.at[layer], wdn_ref, dn_stage, dn_sem)])

    @pl.when(pl.program_id(1) == 0)
    def _():
        carry_ref[...] = jnp.zeros_like(carry_ref)

    shift, scale, gate_mod = mod_ref[0, 3:4, :], mod_ref[0, 4:5, :], mod_ref[0, 5:6, :]
    row = lax.broadcasted_iota(jnp.int32, (sub, FC), 0)

    def conv(slot, cols):
        u = u_ref[slot, :, cols]
        prev1 = carry_ref[V7X_SUBLANES - 1:V7X_SUBLANES, cols]
        prev2 = carry_ref[V7X_SUBLANES - 2:V7X_SUBLANES - 1, cols]
        u1 = jnp.where(row == 0, prev1, pltpu.roll(u, 1, 0))
        u2 = jnp.where(row == 0, prev2, jnp.where(row == 1, prev1, pltpu.roll(u, 2, 0)))
        return (cb_ref[0:1, cols] + cw_ref[0:1, cols] * u2 + cw_ref[1:2, cols] * u1
                + cw_ref[2:3, cols] * u)

    def up(s):
        rows = slice(s * sub, (s + 1) * sub)
        h = _adaln(x_ref[0, rows, :], g_ref[2:3, :], shift, scale).astype(BF16)
        u_ref[s % 2] = _dot(h, wup_ref[...])

    def activate(s):
        slot = s % 2
        for c in range(F // FC):
            gate = conv(slot, slice(c * FC, (c + 1) * FC))
            val = conv(slot, slice(F + c * FC, F + (c + 1) * FC))
            act_ref[slot, :, c * FC:(c + 1) * FC] = (gate * _sigmoid(gate) * val).astype(BF16)
        carry_ref[...] = u_ref[slot, sub - V7X_SUBLANES:sub, :]

    def down(s):
        rows = slice(s * sub, (s + 1) * sub)
        y = _dot(act_ref[s % 2], wdn_ref[...])
        o_ref[0, rows, :] = x_ref[0, rows, :] + _gated_rmsnorm(y, g_ref[3:4, :], gate_mod)

    n_sub = tm // sub
    up(0)
    for s in range(n_sub):
        if s + 1 < n_sub:
            up(s + 1)
        activate(s)
        down(s)


def _ffn_layer(x, mod, g, w_up, conv_w, conv_b, w_down, l, tm, sub):
    B, S, Dm = x.shape
    F2 = w_up.shape[-1]
    row_spec = pl.BlockSpec((1, tm, Dm), lambda b, i: (b, i, 0))
    return pl.pallas_call(
        functools.partial(_ffn_kernel, tm=tm, sub=sub, layer=l),
        grid=(B, S // tm),
        in_specs=[
            row_spec,
            _mod_spec(mod, l),
            _resident_layer(g, l),
            pl.BlockSpec(memory_space=pl.ANY),
            _resident_layer(conv_w, l),
            _resident_layer(conv_b, l),
            pl.BlockSpec(memory_space=pl.ANY),
        ],
        out_specs=row_spec,
        out_shape=jax.ShapeDtypeStruct(x.shape, F32),
        scratch_shapes=[
            pltpu.VMEM((2, sub, F2), F32),
            pltpu.VMEM((2, sub, F2 // 2), BF16),
            pltpu.VMEM((V7X_SUBLANES, F2), F32),
            pltpu.VMEM((Dm, F2), BF16),
            pltpu.VMEM((F2 // 2, Dm), BF16),
            pltpu.VMEM((2, FFN_UP_STAGE_ROWS, F2), F32),
            pltpu.VMEM((2, FFN_DOWN_STAGE_ROWS, Dm), F32),
            pltpu.SemaphoreType.DMA((2,)),
            pltpu.SemaphoreType.DMA((2,)),
        ],
        compiler_params=_params(),
        name="conv_ffn_sublayer",
    )(x, mod, g, w_up, conv_w, conv_b, w_down)


def _ret_kernel(x_ref, mod_ref, g_ref, win_hbm, wout_hbm, cos_ref, sin_ref, dmask_ref,
                xi_ref, zeta_ref, gch_ref, o_ref, proj_ref, go_ref, state_ref,
                win_ref, wout_ref, in_stage, out_stage, in_sem, out_sem, *, tm, layer):
    H, dk, dv, C = RET_HEADS, RET_QK_DIM, RET_V_DIM, RET_KERNEL_CHUNK
    half = dk // 2
    k_off, v_off, g_off = H * dk, 2 * H * dk, 2 * H * dk + H * dv

    @pl.when(jnp.logical_and(pl.program_id(0) == 0, pl.program_id(1) == 0))
    def _():
        _load_weights_bf16([_WeightLoad(win_hbm.at[layer], win_ref, in_stage, in_sem),
                            _WeightLoad(wout_hbm.at[layer], wout_ref, out_stage, out_sem)])

    @pl.when(pl.program_id(1) == 0)
    def _():
        state_ref[...] = jnp.zeros_like(state_ref)

    shift, scale, gate_mod = mod_ref[0, 0:1, :], mod_ref[0, 1:2, :], mod_ref[0, 2:3, :]

    def rotary(t, cos, sin):
        t1, t2 = t[:, :half], t[:, half:]
        return jnp.concatenate([t1 * cos - t2 * sin, t2 * cos + t1 * sin], axis=-1)

    def project(rows):
        h = _adaln(x_ref[0, rows, :], g_ref[0:1, :], shift, scale).astype(BF16)
        proj_ref[rows, :] = _dot(h, win_ref[...])

    heads = range(H)

    def core(rows):
        cos, sin = cos_ref[rows, :], sin_ref[rows, :]
        q = [rotary(proj_ref[rows, hd * dk:(hd + 1) * dk], cos, sin).astype(BF16) for hd in heads]
        k = [(rotary(proj_ref[rows, k_off + hd * dk:k_off + (hd + 1) * dk], cos, sin)
              * (dk ** -0.5)).astype(BF16) for hd in heads]
        v = [proj_ref[rows, v_off + hd * dv:v_off + (hd + 1) * dv] for hd in heads]
        s = [_dot_nt(q[hd], k[hd]) for hd in heads]
        s = [(s[hd] * dmask_ref[hd]).astype(BF16) for hd in heads]
        inner = [_dot(s[hd], v[hd].astype(BF16)) for hd in heads]
        state = [state_ref[hd] for hd in heads]
        cross = [_dot(q[hd], state[hd].astype(BF16)) for hd in heads]
        kv = [_dot_tn(k[hd], (v[hd] * zeta_ref[hd]).astype(BF16)) for hd in heads]
        for hd in heads:
            state_ref[hd] = state[hd] * gch_ref[hd] + kv[hd]
        for hd in heads:
            for r0 in range(0, C, RET_NORM_ROWS):
                rs = slice(r0, r0 + RET_NORM_ROWS)
                out_rows = slice(rows.start + r0, rows.start + r0 + RET_NORM_ROWS)
                o = inner[hd][rs] + cross[hd][rs] * xi_ref[hd, rs, :]
                mu = jnp.mean(o, axis=-1, keepdims=True)
                oc = o - mu
                var = jnp.mean(oc * oc, axis=-1, keepdims=True)
                on = oc * lax.rsqrt(var + NORM_EPS)
                gate = proj_ref[out_rows, g_off + hd * dv:g_off + (hd + 1) * dv]
                go_ref[out_rows, hd * dv:(hd + 1) * dv] = (
                    gate * _sigmoid(gate) * on).astype(BF16)

    def finish(rows):
        y = _dot(go_ref[rows, :], wout_ref[...])
        o_ref[0, rows, :] = x_ref[0, rows, :] + _gated_rmsnorm(y, g_ref[1:2, :], gate_mod)

    chunks = [slice(c * C, (c + 1) * C) for c in range(tm // C)]
    for rows in chunks:
        project(rows)
    for rows in chunks:
        core(rows)
    for rows in chunks:
        finish(rows)


def _retention_tables():
    H, C, dv = RET_HEADS, RET_KERNEL_CHUNK, RET_V_DIM
    log_gamma = jnp.log(1.0 - 2.0 ** (-5.0 - jnp.arange(H, dtype=F32)))
    idx = jnp.arange(C, dtype=F32)
    diff = idx[:, None] - idx[None, :]
    dmask = jnp.where(diff[None] >= 0,
                      jnp.exp(jnp.maximum(diff, 0.0)[None] * log_gamma[:, None, None]), 0.0)
    xi = jnp.exp((idx[None, :] + 1.0) * log_gamma[:, None])
    zeta = jnp.exp((C - 1.0 - idx[None, :]) * log_gamma[:, None])
    g_chunk = jnp.exp(C * log_gamma)
    xi_b = jnp.broadcast_to(xi[:, :, None], (H, C, dv))
    zeta_b = jnp.broadcast_to(zeta[:, :, None], (H, C, dv))
    return dmask, xi_b, zeta_b, g_chunk


def _rotary_tables(S):
    half = RET_QK_DIM // 2
    inv = ROPE_BASE ** (-jnp.arange(half, dtype=F32) / half)
    hi = (jnp.arange(S // ROT_SPLIT) * ROT_SPLIT).astype(F32)[:, None, None] * inv
    lo = jnp.arange(ROT_SPLIT).astype(F32)[None, :, None] * inv
    cos = jnp.cos(hi) * jnp.cos(lo) - jnp.sin(hi) * jnp.sin(lo)
    sin = jnp.sin(hi) * jnp.cos(lo) + jnp.cos(hi) * jnp.sin(lo)
    return cos.reshape(S, half), sin.reshape(S, half)


def _retention_layer(x, mod, g, w_in, w_out, cos, sin, tables, l, li, tm):
    B, S, Dm = x.shape
    H, dk, dv, C = RET_HEADS, RET_QK_DIM, RET_V_DIM, RET_KERNEL_CHUNK
    dmask, xi_b, zeta_b, g_chunk = tables
    row_spec = pl.BlockSpec((1, tm, Dm), lambda b, i: (b, i, 0))
    pos_spec = pl.BlockSpec((tm, dk // 2), lambda b, i: (i, 0))
    return pl.pallas_call(
        functools.partial(_ret_kernel, tm=tm, layer=li),
        grid=(B, S // tm),
        in_specs=[
            row_spec,
            _mod_spec(mod, l),
            _resident_layer(g, l),
            pl.BlockSpec(memory_space=pl.ANY),
            pl.BlockSpec(memory_space=pl.ANY),
            pos_spec,
            pos_spec,
            _resident((H, C, C)),
            _resident((H, C, dv)),
            _resident((H, C, dv)),
            pl.BlockSpec(memory_space=pltpu.SMEM),
        ],
        out_specs=row_spec,
        out_shape=jax.ShapeDtypeStruct(x.shape, F32),
        scratch_shapes=[
            pltpu.VMEM((tm, w_in.shape[-1]), F32),
            pltpu.VMEM((tm, H * dv), BF16),
            pltpu.VMEM((H, dk, dv), F32),
            pltpu.VMEM(w_in.shape[1:], BF16),
            pltpu.VMEM(w_out.shape[1:], BF16),
            pltpu.VMEM((2, RET_IN_STAGE_ROWS, w_in.shape[-1]), F32),
            pltpu.VMEM((2, RET_OUT_STAGE_ROWS, w_out.shape[-1]), F32),
            pltpu.SemaphoreType.DMA((2,)),
            pltpu.SemaphoreType.DMA((2,)),
        ],
        compiler_params=_params(),
        name="retention_sublayer",
    )(x, mod, g, w_in, w_out, cos, sin, dmask, xi_b, zeta_b, g_chunk)


def _kv_kernel(x_ref, mod_ref, g_ref, wk_ref, wvt_ref, k_ref, vt_ref):
    h = _adaln(x_ref[0], g_ref[...], mod_ref[0, 0:1, :], mod_ref[0, 1:2, :]).astype(BF16)
    k_ref[0] = _dot(h, wk_ref[...]).astype(BF16)
    vt_ref[0] = _dot_nt(wvt_ref[...], h).astype(BF16)


def _shared_kv(x, kv_mod, g, wk_pad, wv_t, tm):
    B, S, Dm = x.shape
    nk, nv = wk_pad.shape[1], wv_t.shape[0]
    return pl.pallas_call(
        _kv_kernel,
        grid=(B, S // tm),
        in_specs=[
            pl.BlockSpec((1, tm, Dm), lambda b, i: (b, i, 0)),
            pl.BlockSpec((1, 2, Dm), lambda b, i: (b, 0, 0)),
            _resident((1, Dm)),
            _resident(wk_pad.shape),
            _resident(wv_t.shape),
        ],
        out_specs=[pl.BlockSpec((1, tm, nk), lambda b, i: (b, i, 0)),
                   pl.BlockSpec((1, nv, tm), lambda b, i: (b, 0, i))],
        out_shape=[jax.ShapeDtypeStruct((B, S, nk), BF16),
                   jax.ShapeDtypeStruct((B, nv, S), BF16)],
        compiler_params=_params(),
        name="shared_kv",
    )(x, kv_mod, g, wk_pad, wv_t)


def _bias_kernel(table_ref, bucket_ref, win_ref, o_ref):
    bucket = bucket_ref[...]
    in_window = win_ref[...] != 0
    for hd in range(SWA_HEADS):
        acc = jnp.zeros(bucket.shape, F32)
        for b in range(REL_BUCKETS):
            acc = jnp.where(bucket == b, table_ref[b, hd], acc)
        o_ref[hd] = jnp.where(in_window, acc, MASK_NEG)


def _bias_band_t(rel_bias):
    i = np.arange(BLOCK)[None, :]
    j = np.arange(2 * BLOCK)[:, None]
    dist = i + BLOCK - j
    n = np.maximum(dist, 0)
    max_exact = REL_BUCKETS // 2
    large = max_exact + (np.log(np.maximum(n, 1).astype(np.float32) / max_exact)
                         / math.log(REL_MAX_DIST / max_exact)
                         * (REL_BUCKETS - max_exact)).astype(np.int32)
    large = np.minimum(large, REL_BUCKETS - 1)
    bucket = np.where(n < max_exact, n, large).astype(np.int32)
    win = ((dist >= 0) & (dist < WINDOW)).astype(np.int32)
    full = pl.BlockSpec((2 * BLOCK, BLOCK), lambda i: (0, 0))
    return pl.pallas_call(
        _bias_kernel,
        grid=(1,),
        in_specs=[pl.BlockSpec(memory_space=pltpu.SMEM), full, full],
        out_specs=pl.BlockSpec((SWA_HEADS, 2 * BLOCK, BLOCK), lambda i: (0, 0, 0)),
        out_shape=jax.ShapeDtypeStruct((SWA_HEADS, 2 * BLOCK, BLOCK), F32),
        name="rel_bias_band",
    )(rel_bias, jnp.asarray(bucket), jnp.asarray(win))


def _swa_kernel(x_ref, mod_ref, g_ref, wqt_ref, wo_ref, kc_ref, kp_ref, vtc_ref, vtp_ref,
                bias_ref, sink_ref, o_ref, qt_ref, att_ref, *, tm, sub):
    KV, G, hd, KL = SWA_KV_HEADS, SWA_GROUP, SWA_HEAD_DIM, K_PAD_LANES
    n_q = SWA_HEADS * hd
    first_tile = pl.program_id(1) == 0
    shift, scale, gate_mod = mod_ref[0, 0:1, :], mod_ref[0, 1:2, :], mod_ref[0, 2:3, :]

    qt_ref[n_q:n_q + hd, :] = jnp.zeros((hd, tm), BF16)
    key_row = lax.broadcasted_iota(jnp.int32, (2 * BLOCK, GQ_ROWS), 0)
    edge_mask = jnp.where(jnp.logical_and(first_tile, key_row < BLOCK), MASK_NEG, 0.0)

    def scores(j, kh):
        cols = slice(j * BLOCK, (j + 1) * BLOCK)
        klanes = slice(kh * KL, (kh + 1) * KL)
        if j == 0:
            keys = jnp.concatenate([kp_ref[0, :, klanes], kc_ref[0, cols, klanes]], axis=0)
        else:
            keys = kc_ref[0, (j - 1) * BLOCK:(j + 1) * BLOCK, klanes]
        q_cat = jnp.concatenate(
            [qt_ref[(kh * G + g) * hd:(kh * G + g) * hd + KL, cols] for g in range(G)],
            axis=1)
        s = _dot(keys, q_cat) + bias_ref[kh]
        return s + edge_mask if j == 0 else s

    def weighted_values(j, kh, p):
        cols = slice(j * BLOCK, (j + 1) * BLOCK)
        vrows = slice(kh * hd, (kh + 1) * hd)
        if j == 0:
            vals_t = jnp.concatenate([vtp_ref[0, vrows, :], vtc_ref[0, vrows, cols]], axis=1)
        else:
            vals_t = vtc_ref[0, vrows, (j - 1) * BLOCK:(j + 1) * BLOCK]
        o_t = _dot(vals_t, p)
        for g in range(G):
            att_ref[(kh * G + g) * hd:(kh * G + g + 1) * hd, cols] = (
                o_t[:, g * BLOCK:(g + 1) * BLOCK].astype(BF16))

    def slabs_of(s):
        nb = sub // BLOCK
        return [(j, kh) for j in range(s * nb, (s + 1) * nb) for kh in range(KV)]

    def softmax_values(s, sc):
        slabs = slabs_of(s)
        m = [jnp.maximum(jnp.max(si, axis=0, keepdims=True), sink_ref[kh])
             for si, (_, kh) in zip(sc, slabs)]
        e = [jnp.exp(si - mi) for si, mi in zip(sc, m)]
        inv = [1.0 / (jnp.sum(ei, axis=0, keepdims=True) + jnp.exp(sink_ref[kh] - mi))
               for ei, mi, (_, kh) in zip(e, m, slabs)]
        for ei, ii, (j, kh) in zip(e, inv, slabs):
            weighted_values(j, kh, (ei * ii).astype(BF16))

    def project_q(rows):
        h = _adaln(x_ref[0, rows, :], g_ref[0:1, :], shift, scale).astype(BF16)
        qt_ref[0:n_q, rows] = (_dot_nt(wqt_ref[...], h) * (hd ** -0.5)).astype(BF16)

    def project_o(rows):
        y = _dot_tn(att_ref[:, rows], wo_ref[...])
        o_ref[0, rows, :] = x_ref[0, rows, :] + _gated_rmsnorm(y, g_ref[1:2, :], gate_mod)

    subs = [slice(s * sub, (s + 1) * sub) for s in range(tm // sub)]
    project_q(subs[0])
    for s in range(len(subs)):
        sc = [scores(j, kh) for j, kh in slabs_of(s)]
        if s + 1 < len(subs):
            project_q(subs[s + 1])
        if s > 0:
            project_o(subs[s - 1])
        softmax_values(s, sc)
    project_o(subs[-1])


def _swa_layer(x, mod, g, w_q_t, w_o, k_pad, v_t, bias_t, sink_rows, l, li, tm, sub):
    B, S, Dm = x.shape
    nk, nv = k_pad.shape[-1], v_t.shape[1]
    n_q = w_q_t.shape[1]
    nb = tm // BLOCK
    row_spec = pl.BlockSpec((1, tm, Dm), lambda b, i: (b, i, 0))
    prev_blk = lambda i: jnp.maximum(i * nb - 1, 0)
    return pl.pallas_call(
        functools.partial(_swa_kernel, tm=tm, sub=sub),
        grid=(B, S // tm),
        in_specs=[
            row_spec,
            _mod_spec(mod, l),
            _resident_layer(g, l),
            _resident_layer(w_q_t, li),
            _resident_layer(w_o, li),
            pl.BlockSpec((1, tm, nk), lambda b, i: (b, i, 0)),
            pl.BlockSpec((1, BLOCK, nk), lambda b, i: (b, prev_blk(i), 0)),
            pl.BlockSpec((1, nv, tm), lambda b, i: (b, 0, i)),
            pl.BlockSpec((1, nv, BLOCK), lambda b, i: (b, 0, prev_blk(i))),
            _resident(bias_t.shape),
            _resident_layer(sink_rows, li),
        ],
        out_specs=row_spec,
        out_shape=jax.ShapeDtypeStruct(x.shape, F32),
        scratch_shapes=[
            pltpu.VMEM((n_q + SWA_HEAD_DIM, tm), BF16),
            pltpu.VMEM((n_q, tm), BF16),
        ],
        compiler_params=_params(),
        name="swa_sublayer",
    )(x, mod, g, w_q_t, w_o, k_pad, k_pad, v_t, v_t, bias_t, sink_rows)


def _kv_weights(kv_w):
    Dm = kv_w.shape[0]
    n = SWA_KV_HEADS * SWA_HEAD_DIM
    wk = kv_w[:, :n].reshape(Dm, SWA_KV_HEADS, SWA_HEAD_DIM)
    wk = jnp.pad(wk, ((0, 0), (0, 0), (0, K_PAD_LANES - SWA_HEAD_DIM)))
    return wk.reshape(Dm, SWA_KV_HEADS * K_PAD_LANES).astype(BF16), kv_w[:, n:].T.astype(BF16)


class _Tiles(NamedTuple):
    ffn: int
    ffn_sub: int
    retention: int
    swa: int
    swa_sub: int
    kv: int


def _tile_plan(S):
    sub = V7X_MXU_DIM
    plan = _Tiles(ffn=2 * sub, ffn_sub=sub, retention=2 * RET_KERNEL_CHUNK,
                  swa=4 * sub, swa_sub=2 * sub, kv=4 * sub)
    assert all(S % t == 0 for t in plan), (S, plan)
    return plan


def kernel(x, c, norm_g, ada_w, ada_b, ret_w_in, ret_w_out, kv_norm_g, kv_ada_w, kv_ada_b, kv_w,
           swa_w_q, swa_w_o, swa_sinks, rel_bias, ffn_w_up, ffn_conv_w, ffn_conv_b, ffn_w_down):
    B, S, Dm = x.shape
    tiles = _tile_plan(S)

    c_pad = jnp.zeros((V7X_SUBLANES, Dm), F32).at[:B].set(c)
    mod = _modulation(c_pad, ada_w, ada_b[:, None, :], MOD_COL_TILE)
    mod = mod[:, :B].reshape(DEPTH, B, 6, Dm)
    kv_mod = _modulation(c_pad, kv_ada_w[None], kv_ada_b[None, None, :], MOD_COL_TILE)
    kv_mod = kv_mod[0, :B].reshape(B, 2, Dm)

    cos, sin = _rotary_tables(S)
    tables = _retention_tables()
    bias_t = _bias_band_t(rel_bias).reshape(SWA_KV_HEADS, SWA_GROUP, 2 * BLOCK, BLOCK)
    bias_t = bias_t.transpose(0, 2, 1, 3).reshape(SWA_KV_HEADS, 2 * BLOCK, GQ_ROWS)
    sink_rows = jnp.repeat(swa_sinks.astype(F32), BLOCK, axis=1).reshape(
        swa_sinks.shape[0], SWA_KV_HEADS, 1, GQ_ROWS)

    swa_w_q_t, swa_w_o = swa_w_q.transpose(0, 2, 1).astype(BF16), swa_w_o.astype(BF16)
    ffn_conv_b = ffn_conv_b[:, None, :]

    k_pad = v_t = None
    for l in range(DEPTH):
        if l < N_A_LAYERS:
            x = _retention_layer(x, mod, norm_g, ret_w_in, ret_w_out, cos, sin, tables, l, l,
                                 tiles.retention)
        else:
            x = _swa_layer(x, mod, norm_g, swa_w_q_t, swa_w_o, k_pad, v_t, bias_t, sink_rows,
                           l, l - N_A_LAYERS, tiles.swa, tiles.swa_sub)
        x = _ffn_layer(x, mod, norm_g, ffn_w_up, ffn_conv_w, ffn_conv_b, ffn_w_down,
                       l, tiles.ffn, tiles.ffn_sub)
        if l == N_A_LAYERS - 1:
            wk_pad, wv_t = _kv_weights(kv_w)
            k_pad, v_t = _shared_kv(x, kv_mod, kv_norm_g[None, :], wk_pad, wv_t, tiles.kv)
    return x
```

```python
import functools
import math
from typing import NamedTuple

import jax
import jax.numpy as jnp
import numpy as np
from jax import lax
from jax.experimental import pallas as pl
from jax.experimental.pallas import tpu as pltpu

D_MODEL = 1024
DEPTH = 4
N_A_LAYERS = DEPTH // 2
RET_HEADS = 4
RET_QK_DIM = D_MODEL // RET_HEADS
RET_V_DIM = 2 * D_MODEL // RET_HEADS
ROPE_BASE = 10000.0
SWA_HEADS = 16
SWA_KV_HEADS = 4
SWA_GROUP = SWA_HEADS // SWA_KV_HEADS
SWA_HEAD_DIM = 64
WINDOW = 128
BLOCK = WINDOW
REL_BUCKETS = 32
REL_MAX_DIST = 128
D_FF = 2816
NORM_EPS = 1e-6

V7X_SUBLANES = 8
V7X_LANES = 128
V7X_MXU_DIM = 256
V7X_VMEM_BYTES = 64 * 1024 * 1024
V7X_VMEM_LIMIT_BYTES = V7X_VMEM_BYTES - 6 * 1024 * 1024
MOD_COL_TILE = 2048

MASK_NEG = -1e30
BF16 = jnp.bfloat16
F32 = jnp.float32

RET_KERNEL_CHUNK = V7X_MXU_DIM
RET_NORM_ROWS = 32
FFN_COL_CHUNK = V7X_MXU_DIM
FFN_UP_STAGE_ROWS, FFN_DOWN_STAGE_ROWS = 128, 704
RET_IN_STAGE_ROWS, RET_OUT_STAGE_ROWS = 64, 256
ROT_SPLIT = 128
GQ_ROWS = SWA_GROUP * BLOCK
K_PAD_LANES = V7X_LANES


def _dot(a, b):
    return jnp.dot(a, b, preferred_element_type=F32)


def _dot_nt(a, b):
    return lax.dot_general(a, b, (((1,), (1,)), ((), ())), preferred_element_type=F32)


def _dot_tn(a, b):
    return lax.dot_general(a, b, (((0,), (0,)), ((), ())), preferred_element_type=F32)


def _sigmoid(x):
    return 1.0 / (1.0 + jnp.exp(-x))


def _adaln(x, g, shift, scale):
    ms = jnp.mean(x * x, axis=-1, keepdims=True)
    return x * lax.rsqrt(ms + NORM_EPS) * (g * (1.0 + scale)) + shift


def _gated_rmsnorm(y, g, gate):
    ms = jnp.mean(y * y, axis=-1, keepdims=True)
    return y * lax.rsqrt(ms + NORM_EPS) * (g * (1.0 + gate))


def _resident(shape):
    zeros = (0,) * len(shape)
    return pl.BlockSpec(shape, lambda b, i: zeros, pipeline_mode=pl.Buffered(1))


class _WeightLoad(NamedTuple):
    hbm: object
    bf16: object
    stage: object
    sem: object


def _load_weights_bf16(loads):
    def rows(ld):
        return ld.stage.shape[1]

    def n_chunks(ld):
        return ld.bf16.shape[0] // rows(ld)

    def chunk_copy(ld, k):
        return pltpu.make_async_copy(
            ld.hbm.at[pl.ds(k * rows(ld), rows(ld)), :], ld.stage.at[k % 2], ld.sem.at[k % 2])

    for ld in loads:
        chunk_copy(ld, 0).start()
    for k in range(max(n_chunks(ld) for ld in loads)):
        for ld in loads:
            if k < n_chunks(ld):
                if k + 1 < n_chunks(ld):
                    chunk_copy(ld, k + 1).start()
                chunk_copy(ld, k).wait()
                ld.bf16[k * rows(ld):(k + 1) * rows(ld), :] = ld.stage[k % 2].astype(BF16)


def _resident_layer(stacked, l):
    tail = (0,) * (stacked.ndim - 1)
    return pl.BlockSpec((None,) + stacked.shape[1:], lambda b, i: (l,) + tail,
                        pipeline_mode=pl.Buffered(1))


def _mod_spec(mod, l):
    return pl.BlockSpec((None, 1) + mod.shape[2:], lambda b, i: (l, b, 0, 0))


def _params():
    return pltpu.CompilerParams(
        dimension_semantics=("arbitrary", "arbitrary"),
        vmem_limit_bytes=V7X_VMEM_LIMIT_BYTES)


def _mod_kernel(c_ref, w_ref, b_ref, o_ref):
    c = c_ref[...]
    ca = (c * _sigmoid(c)).astype(BF16)
    o_ref[0] = _dot(ca, w_ref[0].astype(BF16)) + b_ref[0]


def _modulation(c_pad, w, b, tn):
    L, Dm, N = w.shape
    return pl.pallas_call(
        _mod_kernel,
        grid=(L, N // tn),
        in_specs=[
            pl.BlockSpec((V7X_SUBLANES, Dm), lambda l, j: (0, 0)),
            pl.BlockSpec((1, Dm, tn), lambda l, j: (l, 0, j)),
            pl.BlockSpec((1, 1, tn), lambda l, j: (l, 0, j)),
        ],
        out_specs=pl.BlockSpec((1, V7X_SUBLANES, tn), lambda l, j: (l, 0, j)),
        out_shape=jax.ShapeDtypeStruct((L, V7X_SUBLANES, N), F32),
        compiler_params=_params(),
        name="adaln_modulation",
    )(c_pad, w, b)


def _ffn_kernel(x_ref, mod_ref, g_ref, wup_hbm, cw_ref, cb_ref, wdn_hbm, o_ref,
                u_ref, act_ref, carry_ref, wup_ref, wdn_ref, up_stage, dn_stage, up_sem, dn_sem,
                *, tm, sub, layer):
    F = D_FF
    FC = FFN_COL_CHUNK

    @pl.when(jnp.logical_and(pl.program_id(0) == 0, pl.program_id(1) == 0))
    def _():
        _load_weights_bf16([_WeightLoad(wup_hbm.at[layer], wup_ref, up_stage, up_sem),
                            _WeightLoad(wdn_hbm.at[layer], wdn_ref, dn_stage, dn_sem)])

    @pl.when(pl.program_id(1) == 0)
    def _():
        carry_ref[...] = jnp.zeros_like(carry_ref)

    shift, scale, gate_mod = mod_ref[0, 3:4, :], mod_ref[0, 4:5, :], mod_ref[0, 5:6, :]
    row = lax.broadcasted_iota(jnp.int32, (sub, FC), 0)

    def conv(slot, cols):
        u = u_ref[slot, :, cols]
        prev1 = carry_ref[V7X_SUBLANES - 1:V7X_SUBLANES, cols]
        prev2 = carry_ref[V7X_SUBLANES - 2:V7X_SUBLANES - 1, cols]
        u1 = jnp.where(row == 0, prev1, pltpu.roll(u, 1, 0))
        u2 = jnp.where(row == 0, prev2, jnp.where(row == 1, prev1, pltpu.roll(u, 2, 0)))
        return (cb_ref[0:1, cols] + cw_ref[0:1, cols] * u2 + cw_ref[1:2, cols] * u1
                + cw_ref[2:3, cols] * u)

    def up(s):
        rows = slice(s * sub, (s + 1) * sub)
        h = _adaln(x_ref[0, rows, :], g_ref[2:3, :], shift, scale).astype(BF16)
        u_ref[s % 2] = _dot(h, wup_ref[...])

    def activate(s):
        slot = s % 2
        for c in range(F // FC):
            gate = conv(slot, slice(c * FC, (c + 1) * FC))
            val = conv(slot, slice(F + c * FC, F + (c + 1) * FC))
            act_ref[slot, :, c * FC:(c + 1) * FC] = (gate * _sigmoid(gate) * val).astype(BF16)
        carry_ref[...] = u_ref[slot, sub - V7X_SUBLANES:sub, :]

    def down(s):
        rows = slice(s * sub, (s + 1) * sub)
        y = _dot(act_ref[s % 2], wdn_ref[...])
        o_ref[0, rows, :] = x_ref[0, rows, :] + _gated_rmsnorm(y, g_ref[3:4, :], gate_mod)

    n_sub = tm // sub
    up(0)
    for s in range(n_sub):
        if s + 1 < n_sub:
            up(s + 1)
        activate(s)
        down(s)


def _ffn_layer(x, mod, g, w_up, conv_w, conv_b, w_down, l, tm, sub):
    B, S, Dm = x.shape
    F2 = w_up.shape[-1]
    row_spec = pl.BlockSpec((1, tm, Dm), lambda b, i: (b, i, 0))
    return pl.pallas_call(
        functools.partial(_ffn_kernel, tm=tm, sub=sub, layer=l),
        grid=(B, S // tm),
        in_specs=[
            row_spec,
            _mod_spec(mod, l),
            _resident_layer(g, l),
            pl.BlockSpec(memory_space=pl.ANY),
            _resident_layer(conv_w, l),
            _resident_layer(conv_b, l),
            pl.BlockSpec(memory_space=pl.ANY),
        ],
        out_specs=row_spec,
        out_shape=jax.ShapeDtypeStruct(x.shape, F32),
        scratch_shapes=[
            pltpu.VMEM((2, sub, F2), F32),
            pltpu.VMEM((2, sub, F2 // 2), BF16),
            pltpu.VMEM((V7X_SUBLANES, F2), F32),
            pltpu.VMEM((Dm, F2), BF16),
            pltpu.VMEM((F2 // 2, Dm), BF16),
            pltpu.VMEM((2, FFN_UP_STAGE_ROWS, F2), F32),
            pltpu.VMEM((2, FFN_DOWN_STAGE_ROWS, Dm), F32),
            pltpu.SemaphoreType.DMA((2,)),
            pltpu.SemaphoreType.DMA((2,)),
        ],
        compiler_params=_params(),
        name="conv_ffn_sublayer",
    )(x, mod, g, w_up, conv_w, conv_b, w_down)


def _ret_kernel(x_ref, mod_ref, g_ref, win_hbm, wout_hbm, cos_ref, sin_ref, dmask_ref,
                xi_ref, zeta_ref, gch_ref, o_ref, proj_ref, go_ref, state_ref,
                win_ref, wout_ref, in_stage, out_stage, in_sem, out_sem, *, tm, layer):
    H, dk, dv, C = RET_HEADS, RET_QK_DIM, RET_V_DIM, RET_KERNEL_CHUNK
    half = dk // 2
    k_off, v_off, g_off = H * dk, 2 * H * dk, 2 * H * dk + H * dv

    @pl.when(jnp.logical_and(pl.program_id(0) == 0, pl.program_id(1) == 0))
    def _():
        _load_weights_bf16([_WeightLoad(win_hbm.at[layer], win_ref, in_stage, in_sem),
                            _WeightLoad(wout_hbm.at[layer], wout_ref, out_stage, out_sem)])

    @pl.when(pl.program_id(1) == 0)
    def _():
        state_ref[...] = jnp.zeros_like(state_ref)

    shift, scale, gate_mod = mod_ref[0, 0:1, :], mod_ref[0, 1:2, :], mod_ref[0, 2:3, :]

    def rotary(t, cos, sin):
        t1, t2 = t[:, :half], t[:, half:]
        return jnp.concatenate([t1 * cos - t2 * sin, t2 * cos + t1 * sin], axis=-1)

    def project(rows):
        h = _adaln(x_ref[0, rows, :], g_ref[0:1, :], shift, scale).astype(BF16)
        proj_ref[rows, :] = _dot(h, win_ref[...])

    heads = range(H)

    def core(rows):
        cos, sin = cos_ref[rows, :], sin_ref[rows, :]
        q = [rotary(proj_ref[rows, hd * dk:(hd + 1) * dk], cos, sin).astype(BF16) for hd in heads]
        k = [(rotary(proj_ref[rows, k_off + hd * dk:k_off + (hd + 1) * dk], cos, sin)
              * (dk ** -0.5)).astype(BF16) for hd in heads]
        v = [proj_ref[rows, v_off + hd * dv:v_off + (hd + 1) * dv] for hd in heads]
        s = [_dot_nt(q[hd], k[hd]) for hd in heads]
        s = [(s[hd] * dmask_ref[hd]).astype(BF16) for hd in heads]
        inner = [_dot(s[hd], v[hd].astype(BF16)) for hd in heads]
        state = [state_ref[hd] for hd in heads]
        cross = [_dot(q[hd], state[hd].astype(BF16)) for hd in heads]
        kv = [_dot_tn(k[hd], (v[hd] * zeta_ref[hd]).astype(BF16)) for hd in heads]
        for hd in heads:
            state_ref[hd] = state[hd] * gch_ref[hd] + kv[hd]
        for hd in heads:
            for r0 in range(0, C, RET_NORM_ROWS):
                rs = slice(r0, r0 + RET_NORM_ROWS)
                out_rows = slice(rows.start + r0, rows.start + r0 + RET_NORM_ROWS)
                o = inner[hd][rs] + cross[hd][rs] * xi_ref[hd, rs, :]
                mu = jnp.mean(o, axis=-1, keepdims=True)
                oc = o - mu
                var = jnp.mean(oc * oc, axis=-1, keepdims=True)
                on = oc * lax.rsqrt(var + NORM_EPS)
                gate = proj_ref[out_rows, g_off + hd * dv:g_off + (hd + 1) * dv]
                go_ref[out_rows, hd * dv:(hd + 1) * dv] = (
                    gate * _sigmoid(gate) * on).astype(BF16)

    def finish(rows):
        y = _dot(go_ref[rows, :], wout_ref[...])
        o_ref[0, rows, :] = x_ref[0, rows, :] + _gated_rmsnorm(y, g_ref[1:2, :], gate_mod)

    chunks = [slice(c * C, (c + 1) * C) for c in range(tm // C)]
    for rows in chunks:
        project(rows)
    for rows in chunks:
        core(rows)
    for rows in chunks:
        finish(rows)


def _retention_tables():
    H, C, dv = RET_HEADS, RET_KERNEL_CHUNK, RET_V_DIM
    log_gamma = jnp.log(1.0 - 2.0 ** (-5.0 - jnp.arange(H, dtype=F32)))
    idx = jnp.arange(C, dtype=F32)
    diff = idx[:, None] - idx[None, :]
    dmask = jnp.where(diff[None] >= 0,
                      jnp.exp(jnp.maximum(diff, 0.0)[None] * log_gamma[:, None, None]), 0.0)
    xi = jnp.exp((idx[None, :] + 1.0) * log_gamma[:, None])
    zeta = jnp.exp((C - 1.0 - idx[None, :]) * log_gamma[:, None])
    g_chunk = jnp.exp(C * log_gamma)
    xi_b = jnp.broadcast_to(xi[:, :, None], (H, C, dv))
    zeta_b = jnp.broadcast_to(zeta[:, :, None], (H, C, dv))
    return dmask, xi_b, zeta_b, g_chunk


def _rotary_tables(S):
    half = RET_QK_DIM // 2
    inv = ROPE_BASE ** (-jnp.arange(half, dtype=F32) / half)
    hi = (jnp.arange(S // ROT_SPLIT) * ROT_SPLIT).astype(F32)[:, None, None] * inv
    lo = jnp.arange(ROT_SPLIT).astype(F32)[None, :, None] * inv
    cos = jnp.cos(hi) * jnp.cos(lo) - jnp.sin(hi) * jnp.sin(lo)
    sin = jnp.sin(hi) * jnp.cos(lo) + jnp.cos(hi) * jnp.sin(lo)
    return cos.reshape(S, half), sin.reshape(S, half)


def _retention_layer(x, mod, g, w_in, w_out, cos, sin, tables, l, li, tm):
    B, S, Dm = x.shape
    H, dk, dv, C = RET_HEADS, RET_QK_DIM, RET_V_DIM, RET_KERNEL_CHUNK
    dmask, xi_b, zeta_b, g_chunk = tables
    row_spec = pl.BlockSpec((1, tm, Dm), lambda b, i: (b, i, 0))
    pos_spec = pl.BlockSpec((tm, dk // 2), lambda b, i: (i, 0))
    return pl.pallas_call(
        functools.partial(_ret_kernel, tm=tm, layer=li),
        grid=(B, S // tm),
        in_specs=[
            row_spec,
            _mod_spec(mod, l),
            _resident_layer(g, l),
            pl.BlockSpec(memory_space=pl.ANY),
            pl.BlockSpec(memory_space=pl.ANY),
            pos_spec,
            pos_spec,
            _resident((H, C, C)),
            _resident((H, C, dv)),
            _resident((H, C, dv)),
            pl.BlockSpec(memory_space=pltpu.SMEM),
        ],
        out_specs=row_spec,
        out_shape=jax.ShapeDtypeStruct(x.shape, F32),
        scratch_shapes=[
            pltpu.VMEM((tm, w_in.shape[-1]), F32),
            pltpu.VMEM((tm, H * dv), BF16),
            pltpu.VMEM((H, dk, dv), F32),
            pltpu.VMEM(w_in.shape[1:], BF16),
            pltpu.VMEM(w_out.shape[1:], BF16),
            pltpu.VMEM((2, RET_IN_STAGE_ROWS, w_in.shape[-1]), F32),
            pltpu.VMEM((2, RET_OUT_STAGE_ROWS, w_out.shape[-1]), F32),
            pltpu.SemaphoreType.DMA((2,)),
            pltpu.SemaphoreType.DMA((2,)),
        ],
        compiler_params=_params(),
        name="retention_sublayer",
    )(x, mod, g, w_in, w_out, cos, sin, dmask, xi_b, zeta_b, g_chunk)


def _kv_kernel(x_ref, mod_ref, g_ref, wk_ref, wvt_ref, k_ref, vt_ref):
    h = _adaln(x_ref[0], g_ref[...], mod_ref[0, 0:1, :], mod_ref[0, 1:2, :]).astype(BF16)
    k_ref[0] = _dot(h, wk_ref[...]).astype(BF16)
    vt_ref[0] = _dot_nt(wvt_ref[...], h).astype(BF16)


def _shared_kv(x, kv_mod, g, wk_pad, wv_t, tm):
    B, S, Dm = x.shape
    nk, nv = wk_pad.shape[1], wv_t.shape[0]
    return pl.pallas_call(
        _kv_kernel,
        grid=(B, S // tm),
        in_specs=[
            pl.BlockSpec((1, tm, Dm), lambda b, i: (b, i, 0)),
            pl.BlockSpec((1, 2, Dm), lambda b, i: (b, 0, 0)),
            _resident((1, Dm)),
            _resident(wk_pad.shape),
            _resident(wv_t.shape),
        ],
        out_specs=[pl.BlockSpec((1, tm, nk), lambda b, i: (b, i, 0)),
                   pl.BlockSpec((1, nv, tm), lambda b, i: (b, 0, i))],
        out_shape=[jax.ShapeDtypeStruct((B, S, nk), BF16),
                   jax.ShapeDtypeStruct((B, nv, S), BF16)],
        compiler_params=_params(),
        name="shared_kv",
    )(x, kv_mod, g, wk_pad, wv_t)


def _bias_kernel(table_ref, bucket_ref, win_ref, o_ref):
    bucket = bucket_ref[...]
    in_window = win_ref[...] != 0
    for hd in range(SWA_HEADS):
        acc = jnp.zeros(bucket.shape, F32)
        for b in range(REL_BUCKETS):
            acc = jnp.where(bucket == b, table_ref[b, hd], acc)
        o_ref[hd] = jnp.where(in_window, acc, MASK_NEG)


def _bias_band_t(rel_bias):
    i = np.arange(BLOCK)[None, :]
    j = np.arange(2 * BLOCK)[:, None]
    dist = i + BLOCK - j
    n = np.maximum(dist, 0)
    max_exact = REL_BUCKETS // 2
    large = max_exact + (np.log(np.maximum(n, 1).astype(np.float32) / max_exact)
                         / math.log(REL_MAX_DIST / max_exact)
                         * (REL_BUCKETS - max_exact)).astype(np.int32)
    large = np.minimum(large, REL_BUCKETS - 1)
    bucket = np.where(n < max_exact, n, large).astype(np.int32)
    win = ((dist >= 0) & (dist < WINDOW)).astype(np.int32)
    full = pl.BlockSpec((2 * BLOCK, BLOCK), lambda i: (0, 0))
    return pl.pallas_call(
        _bias_kernel,
        grid=(1,),
        in_specs=[pl.BlockSpec(memory_space=pltpu.SMEM), full, full],
        out_specs=pl.BlockSpec((SWA_HEADS, 2 * BLOCK, BLOCK), lambda i: (0, 0, 0)),
        out_shape=jax.ShapeDtypeStruct((SWA_HEADS, 2 * BLOCK, BLOCK), F32),
        name="rel_bias_band",
    )(rel_bias, jnp.asarray(bucket), jnp.asarray(win))


def _swa_kernel(x_ref, mod_ref, g_ref, wqt_ref, wo_ref, kc_ref, kp_ref, vtc_ref, vtp_ref,
                bias_ref, sink_ref, o_ref, qt_ref, att_ref, *, tm, sub):
    KV, G, hd, KL = SWA_KV_HEADS, SWA_GROUP, SWA_HEAD_DIM, K_PAD_LANES
    n_q = SWA_HEADS * hd
    first_tile = pl.program_id(1) == 0
    shift, scale, gate_mod = mod_ref[0, 0:1, :], mod_ref[0, 1:2, :], mod_ref[0, 2:3, :]

    qt_ref[n_q:n_q + hd, :] = jnp.zeros((hd, tm), BF16)
    key_row = lax.broadcasted_iota(jnp.int32, (2 * BLOCK, GQ_ROWS), 0)
    edge_mask = jnp.where(jnp.logical_and(first_tile, key_row < BLOCK), MASK_NEG, 0.0)

    def scores(j, kh):
        cols = slice(j * BLOCK, (j + 1) * BLOCK)
        klanes = slice(kh * KL, (kh + 1) * KL)
        if j == 0:
            keys = jnp.concatenate([kp_ref[0, :, klanes], kc_ref[0, cols, klanes]], axis=0)
        else:
            keys = kc_ref[0, (j - 1) * BLOCK:(j + 1) * BLOCK, klanes]
        q_cat = jnp.concatenate(
            [qt_ref[(kh * G + g) * hd:(kh * G + g) * hd + KL, cols] for g in range(G)],
            axis=1)
        s = _dot(keys, q_cat) + bias_ref[kh]
        return s + edge_mask if j == 0 else s

    def weighted_values(j, kh, e, inv):
        cols = slice(j * BLOCK, (j + 1) * BLOCK)
        vrows = slice(kh * hd, (kh + 1) * hd)
        if j == 0:
            vals_t = jnp.concatenate([vtp_ref[0, vrows, :], vtc_ref[0, vrows, cols]], axis=1)
        else:
            vals_t = vtc_ref[0, vrows, (j - 1) * BLOCK:(j + 1) * BLOCK]
        o_t = _dot(vals_t, e.astype(BF16)) * inv
        for g in range(G):
            att_ref[(kh * G + g) * hd:(kh * G + g + 1) * hd, cols] = (
                o_t[:, g * BLOCK:(g + 1) * BLOCK].astype(BF16))

    def slabs_of(s):
        nb = sub // BLOCK
        return [(j, kh) for j in range(s * nb, (s + 1) * nb) for kh in range(KV)]

    def softmax_values(s, sc):
        slabs = slabs_of(s)
        m = [jnp.maximum(jnp.max(si, axis=0, keepdims=True), sink_ref[kh])
             for si, (_, kh) in zip(sc, slabs)]
        e = [jnp.exp(si - mi) for si, mi in zip(sc, m)]
        inv = [1.0 / (jnp.sum(ei, axis=0, keepdims=True) + jnp.exp(sink_ref[kh] - mi))
               for ei, mi, (_, kh) in zip(e, m, slabs)]
        for ei, ii, (j, kh) in zip(e, inv, slabs):
            weighted_values(j, kh, ei, ii)

    def project_q(rows):
        h = _adaln(x_ref[0, rows, :], g_ref[0:1, :], shift, scale).astype(BF16)
        qt_ref[0:n_q, rows] = (_dot_nt(wqt_ref[...], h) * (hd ** -0.5)).astype(BF16)

    def project_o(rows):
        y = _dot_tn(att_ref[:, rows], wo_ref[...])
        o_ref[0, rows, :] = x_ref[0, rows, :] + _gated_rmsnorm(y, g_ref[1:2, :], gate_mod)

    subs = [slice(s * sub, (s + 1) * sub) for s in range(tm // sub)]
    project_q(subs[0])
    for s in range(len(subs)):
        sc = [scores(j, kh) for j, kh in slabs_of(s)]
        if s + 1 < len(subs):
            project_q(subs[s + 1])
        if s > 0:
            project_o(subs[s - 1])
        softmax_values(s, sc)
    project_o(subs[-1])


def _swa_layer(x, mod, g, w_q_t, w_o, k_pad, v_t, bias_t, sink_rows, l, li, tm, sub):
    B, S, Dm = x.shape
    nk, nv = k_pad.shape[-1], v_t.shape[1]
    n_q = w_q_t.shape[1]
    nb = tm // BLOCK
    row_spec = pl.BlockSpec((1, tm, Dm), lambda b, i: (b, i, 0))
    prev_blk = lambda i: jnp.maximum(i * nb - 1, 0)
    return pl.pallas_call(
        functools.partial(_swa_kernel, tm=tm, sub=sub),
        grid=(B, S // tm),
        in_specs=[
            row_spec,
            _mod_spec(mod, l),
            _resident_layer(g, l),
            _resident_layer(w_q_t, li),
            _resident_layer(w_o, li),
            pl.BlockSpec((1, tm, nk), lambda b, i: (b, i, 0)),
            pl.BlockSpec((1, BLOCK, nk), lambda b, i: (b, prev_blk(i), 0)),
            pl.BlockSpec((1, nv, tm), lambda b, i: (b, 0, i)),
            pl.BlockSpec((1, nv, BLOCK), lambda b, i: (b, 0, prev_blk(i))),
            _resident(bias_t.shape),
            _resident_layer(sink_rows, li),
        ],
        out_specs=row_spec,
        out_shape=jax.ShapeDtypeStruct(x.shape, F32),
        scratch_shapes=[
            pltpu.VMEM((n_q + SWA_HEAD_DIM, tm), BF16),
            pltpu.VMEM((n_q, tm), BF16),
        ],
        compiler_params=_params(),
        name="swa_sublayer",
    )(x, mod, g, w_q_t, w_o, k_pad, k_pad, v_t, v_t, bias_t, sink_rows)


def _kv_weights(kv_w):
    Dm = kv_w.shape[0]
    n = SWA_KV_HEADS * SWA_HEAD_DIM
    wk = kv_w[:, :n].reshape(Dm, SWA_KV_HEADS, SWA_HEAD_DIM)
    wk = jnp.pad(wk, ((0, 0), (0, 0), (0, K_PAD_LANES - SWA_HEAD_DIM)))
    return wk.reshape(Dm, SWA_KV_HEADS * K_PAD_LANES).astype(BF16), kv_w[:, n:].T.astype(BF16)


class _Tiles(NamedTuple):
    ffn: int
    ffn_sub: int
    retention: int
    swa: int
    swa_sub: int
    kv: int


def _tile_plan(S):
    sub = V7X_MXU_DIM
    plan = _Tiles(ffn=2 * sub, ffn_sub=sub, retention=2 * RET_KERNEL_CHUNK,
                  swa=4 * sub, swa_sub=2 * sub, kv=4 * sub)
    assert all(S % t == 0 for t in plan), (S, plan)
    return plan


def kernel(x, c, norm_g, ada_w, ada_b, ret_w_in, ret_w_out, kv_norm_g, kv_ada_w, kv_ada_b, kv_w,
           swa_w_q, swa_w_o, swa_sinks, rel_bias, ffn_w_up, ffn_conv_w, ffn_conv_b, ffn_w_down):
    B, S, Dm = x.shape
    assert B <= V7X_SUBLANES and Dm == D_MODEL, x.shape
    tiles = _tile_plan(S)

    c_pad = jnp.zeros((V7X_SUBLANES, Dm), F32).at[:B].set(c)
    mod = _modulation(c_pad, ada_w, ada_b[:, None, :], MOD_COL_TILE)
    mod = mod[:, :B].reshape(DEPTH, B, 6, Dm)
    kv_mod = _modulation(c_pad, kv_ada_w[None], kv_ada_b[None, None, :], MOD_COL_TILE)
    kv_mod = kv_mod[0, :B].reshape(B, 2, Dm)

    cos, sin = _rotary_tables(S)
    tables = _retention_tables()
    bias_t = _bias_band_t(rel_bias).reshape(SWA_KV_HEADS, SWA_GROUP, 2 * BLOCK, BLOCK)
    bias_t = bias_t.transpose(0, 2, 1, 3).reshape(SWA_KV_HEADS, 2 * BLOCK, GQ_ROWS)
    sink_rows = jnp.repeat(swa_sinks.astype(F32), BLOCK, axis=1).reshape(
        swa_sinks.shape[0], SWA_KV_HEADS, 1, GQ_ROWS)

    swa_w_q_t, swa_w_o = swa_w_q.transpose(0, 2, 1).astype(BF16), swa_w_o.astype(BF16)
    ffn_conv_b = ffn_conv_b[:, None, :]

    k_pad = v_t = None
    for l in range(DEPTH):
        if l < N_A_LAYERS:
            x = _retention_layer(x, mod, norm_g, ret_w_in, ret_w_out, cos, sin, tables, l, l,
                                 tiles.retention)
        else:
            x = _swa_layer(x, mod, norm_g, swa_w_q_t, swa_w_o, k_pad, v_t, bias_t, sink_rows,
                           l, l - N_A_LAYERS, tiles.swa, tiles.swa_sub)
        x = _ffn_layer(x, mod, norm_g, ffn_w_up, ffn_conv_w, ffn_conv_b, ffn_w_down,
                       l, tiles.ffn, tiles.ffn_sub)
        if l == N_A_LAYERS - 1:
            wk_pad, wv_t = _kv_weights(kv_w)
            k_pad, v_t = _shared_kv(x, kv_mod, kv_norm_g[None, :], wk_pad, wv_t, tiles.kv)
    return x
```

```python
import functools
import math
from typing import NamedTuple

import jax
import jax.numpy as jnp
import numpy as np
from jax import lax
from jax.experimental import pallas as pl
from jax.experimental.pallas import tpu as pltpu

D_MODEL = 1024
DEPTH = 4
N_A_LAYERS = DEPTH // 2
RET_HEADS = 4
RET_QK_DIM = D_MODEL // RET_HEADS
RET_V_DIM = 2 * D_MODEL // RET_HEADS
ROPE_BASE = 10000.0
SWA_HEADS = 16
SWA_KV_HEADS = 4
SWA_GROUP = SWA_HEADS // SWA_KV_HEADS
SWA_HEAD_DIM = 64
WINDOW = 128
BLOCK = WINDOW
REL_BUCKETS = 32
REL_MAX_DIST = 128
D_FF = 2816
NORM_EPS = 1e-6

V7X_SUBLANES = 8
V7X_LANES = 128
V7X_MXU_DIM = 256
V7X_VMEM_BYTES = 64 * 1024 * 1024
V7X_VMEM_LIMIT_BYTES = V7X_VMEM_BYTES - 6 * 1024 * 1024
MOD_COL_TILE = 2048

MASK_NEG = -1e30
LOG2_E = math.log2(math.e)
BF16 = jnp.bfloat16
F32 = jnp.float32

RET_KERNEL_CHUNK = V7X_MXU_DIM
RET_NORM_ROWS = 32
FFN_COL_CHUNK = V7X_MXU_DIM
FFN_UP_STAGE_ROWS, FFN_DOWN_STAGE_ROWS = 128, 704
RET_IN_STAGE_ROWS, RET_OUT_STAGE_ROWS = 64, 256
ROT_SPLIT = 128
GQ_ROWS = SWA_GROUP * BLOCK
K_PAD_LANES = V7X_LANES


def _dot(a, b):
    return jnp.dot(a, b, preferred_element_type=F32)


def _dot_nt(a, b):
    return lax.dot_general(a, b, (((1,), (1,)), ((), ())), preferred_element_type=F32)


def _dot_tn(a, b):
    return lax.dot_general(a, b, (((0,), (0,)), ((), ())), preferred_element_type=F32)


def _sigmoid(x):
    return 1.0 / (1.0 + jnp.exp(-x))


def _adaln(x, g, shift, scale):
    ms = jnp.mean(x * x, axis=-1, keepdims=True)
    return x * lax.rsqrt(ms + NORM_EPS) * (g * (1.0 + scale)) + shift


def _gated_rmsnorm(y, g, gate):
    ms = jnp.mean(y * y, axis=-1, keepdims=True)
    return y * lax.rsqrt(ms + NORM_EPS) * (g * (1.0 + gate))


def _resident(shape):
    zeros = (0,) * len(shape)
    return pl.BlockSpec(shape, lambda b, i: zeros, pipeline_mode=pl.Buffered(1))


class _WeightLoad(NamedTuple):
    hbm: object
    bf16: object
    stage: object
    sem: object


def _load_weights_bf16(loads):
    def rows(ld):
        return ld.stage.shape[1]

    def n_chunks(ld):
        return ld.bf16.shape[0] // rows(ld)

    def chunk_copy(ld, k):
        return pltpu.make_async_copy(
            ld.hbm.at[pl.ds(k * rows(ld), rows(ld)), :], ld.stage.at[k % 2], ld.sem.at[k % 2])

    for ld in loads:
        chunk_copy(ld, 0).start()
    for k in range(max(n_chunks(ld) for ld in loads)):
        for ld in loads:
            if k < n_chunks(ld):
                if k + 1 < n_chunks(ld):
                    chunk_copy(ld, k + 1).start()
                chunk_copy(ld, k).wait()
                ld.bf16[k * rows(ld):(k + 1) * rows(ld), :] = ld.stage[k % 2].astype(BF16)


def _resident_layer(stacked, l):
    tail = (0,) * (stacked.ndim - 1)
    return pl.BlockSpec((None,) + stacked.shape[1:], lambda b, i: (l,) + tail,
                        pipeline_mode=pl.Buffered(1))


def _mod_spec(mod, l):
    return pl.BlockSpec((None, 1) + mod.shape[2:], lambda b, i: (l, b, 0, 0))


def _params():
    return pltpu.CompilerParams(
        dimension_semantics=("arbitrary", "arbitrary"),
        vmem_limit_bytes=V7X_VMEM_LIMIT_BYTES)


def _mod_kernel(c_ref, w_ref, b_ref, o_ref):
    c = c_ref[...]
    ca = (c * _sigmoid(c)).astype(BF16)
    o_ref[0] = _dot(ca, w_ref[0].astype(BF16)) + b_ref[0]


def _modulation(c_pad, w, b, tn):
    L, Dm, N = w.shape
    return pl.pallas_call(
        _mod_kernel,
        grid=(L, N // tn),
        in_specs=[
            pl.BlockSpec((V7X_SUBLANES, Dm), lambda l, j: (0, 0)),
            pl.BlockSpec((1, Dm, tn), lambda l, j: (l, 0, j)),
            pl.BlockSpec((1, 1, tn), lambda l, j: (l, 0, j)),
        ],
        out_specs=pl.BlockSpec((1, V7X_SUBLANES, tn), lambda l, j: (l, 0, j)),
        out_shape=jax.ShapeDtypeStruct((L, V7X_SUBLANES, N), F32),
        compiler_params=_params(),
        name="adaln_modulation",
    )(c_pad, w, b)


def _ffn_kernel(x_ref, mod_ref, g_ref, wup_hbm, cw_ref, cb_ref, wdn_hbm, o_ref,
                u_ref, act_ref, carry_ref, wup_ref, wdn_ref, up_stage, dn_stage, up_sem, dn_sem,
                *, tm, sub, layer):
    F = D_FF
    FC = FFN_COL_CHUNK

    @pl.when(jnp.logical_and(pl.program_id(0) == 0, pl.program_id(1) == 0))
    def _():
        _load_weights_bf16([_WeightLoad(wup_hbm.at[layer], wup_ref, up_stage, up_sem),
                            _WeightLoad(wdn_hbm.at[layer], wdn_ref, dn_stage, dn_sem)])

    @pl.when(pl.program_id(1) == 0)
    def _():
        carry_ref[...] = jnp.zeros_like(carry_ref)

    shift, scale, gate_mod = mod_ref[0, 3:4, :], mod_ref[0, 4:5, :], mod_ref[0, 5:6, :]
    row = lax.broadcasted_iota(jnp.int32, (sub, FC), 0)

    def conv(slot, cols):
        u = u_ref[slot, :, cols]
        prev1 = carry_ref[V7X_SUBLANES - 1:V7X_SUBLANES, cols]
        prev2 = carry_ref[V7X_SUBLANES - 2:V7X_SUBLANES - 1, cols]
        u1 = jnp.where(row == 0, prev1, pltpu.roll(u, 1, 0))
        u2 = jnp.where(row == 0, prev2, jnp.where(row == 1, prev1, pltpu.roll(u, 2, 0)))
        return (cb_ref[0:1, cols] + cw_ref[0:1, cols] * u2 + cw_ref[1:2, cols] * u1
                + cw_ref[2:3, cols] * u)

    def up(s):
        rows = slice(s * sub, (s + 1) * sub)
        h = _adaln(x_ref[0, rows, :], g_ref[2:3, :], shift, scale).astype(BF16)
        u_ref[s % 2] = _dot(h, wup_ref[...])

    def activate(s):
        slot = s % 2
        for c in range(F // FC):
            gate = conv(slot, slice(c * FC, (c + 1) * FC))
            val = conv(slot, slice(F + c * FC, F + (c + 1) * FC))
            act_ref[slot, :, c * FC:(c + 1) * FC] = (gate * _sigmoid(gate) * val).astype(BF16)
        carry_ref[...] = u_ref[slot, sub - V7X_SUBLANES:sub, :]

    def down(s):
        rows = slice(s * sub, (s + 1) * sub)
        y = _dot(act_ref[s % 2], wdn_ref[...])
        o_ref[0, rows, :] = x_ref[0, rows, :] + _gated_rmsnorm(y, g_ref[3:4, :], gate_mod)

    n_sub = tm // sub
    up(0)
    for s in range(n_sub):
        if s + 1 < n_sub:
            up(s + 1)
        activate(s)
        down(s)


def _ffn_layer(x, mod, g, w_up, conv_w, conv_b, w_down, l, tm, sub):
    B, S, Dm = x.shape
    F2 = w_up.shape[-1]
    row_spec = pl.BlockSpec((1, tm, Dm), lambda b, i: (b, i, 0))
    return pl.pallas_call(
        functools.partial(_ffn_kernel, tm=tm, sub=sub, layer=l),
        grid=(B, S // tm),
        in_specs=[
            row_spec,
            _mod_spec(mod, l),
            _resident_layer(g, l),
            pl.BlockSpec(memory_space=pl.ANY),
            _resident_layer(conv_w, l),
            _resident_layer(conv_b, l),
            pl.BlockSpec(memory_space=pl.ANY),
        ],
        out_specs=row_spec,
        out_shape=jax.ShapeDtypeStruct(x.shape, F32),
        scratch_shapes=[
            pltpu.VMEM((2, sub, F2), F32),
            pltpu.VMEM((2, sub, F2 // 2), BF16),
            pltpu.VMEM((V7X_SUBLANES, F2), F32),
            pltpu.VMEM((Dm, F2), BF16),
            pltpu.VMEM((F2 // 2, Dm), BF16),
            pltpu.VMEM((2, FFN_UP_STAGE_ROWS, F2), F32),
            pltpu.VMEM((2, FFN_DOWN_STAGE_ROWS, Dm), F32),
            pltpu.SemaphoreType.DMA((2,)),
            pltpu.SemaphoreType.DMA((2,)),
        ],
        compiler_params=_params(),
        name="conv_ffn_sublayer",
    )(x, mod, g, w_up, conv_w, conv_b, w_down)


def _ret_kernel(x_ref, mod_ref, g_ref, win_hbm, wout_hbm, cos_ref, sin_ref, dmask_ref,
                xi_ref, zeta_ref, gch_ref, o_ref, proj_ref, go_ref, state_ref,
                win_ref, wout_ref, in_stage, out_stage, in_sem, out_sem, *, tm, layer):
    H, dk, dv, C = RET_HEADS, RET_QK_DIM, RET_V_DIM, RET_KERNEL_CHUNK
    half = dk // 2
    k_off, v_off, g_off = H * dk, 2 * H * dk, 2 * H * dk + H * dv

    @pl.when(jnp.logical_and(pl.program_id(0) == 0, pl.program_id(1) == 0))
    def _():
        _load_weights_bf16([_WeightLoad(win_hbm.at[layer], win_ref, in_stage, in_sem),
                            _WeightLoad(wout_hbm.at[layer], wout_ref, out_stage, out_sem)])

    @pl.when(pl.program_id(1) == 0)
    def _():
        state_ref[...] = jnp.zeros_like(state_ref)

    shift, scale, gate_mod = mod_ref[0, 0:1, :], mod_ref[0, 1:2, :], mod_ref[0, 2:3, :]

    def rotary(t, cos, sin):
        t1, t2 = t[:, :half], t[:, half:]
        return jnp.concatenate([t1 * cos - t2 * sin, t2 * cos + t1 * sin], axis=-1)

    def project(rows):
        h = _adaln(x_ref[0, rows, :], g_ref[0:1, :], shift, scale).astype(BF16)
        proj_ref[rows, :] = _dot(h, win_ref[...])

    heads = range(H)

    def core(rows):
        cos, sin = cos_ref[rows, :], sin_ref[rows, :]
        q = [rotary(proj_ref[rows, hd * dk:(hd + 1) * dk], cos, sin).astype(BF16) for hd in heads]
        k = [(rotary(proj_ref[rows, k_off + hd * dk:k_off + (hd + 1) * dk], cos, sin)
              * (dk ** -0.5)).astype(BF16) for hd in heads]
        v = [proj_ref[rows, v_off + hd * dv:v_off + (hd + 1) * dv] for hd in heads]
        s = [_dot_nt(q[hd], k[hd]) for hd in heads]
        s = [(s[hd] * dmask_ref[hd]).astype(BF16) for hd in heads]
        inner = [_dot(s[hd], v[hd].astype(BF16)) for hd in heads]
        state = [state_ref[hd] for hd in heads]
        cross = [_dot(q[hd], state[hd].astype(BF16)) for hd in heads]
        kv = [_dot_tn(k[hd], (v[hd] * zeta_ref[hd]).astype(BF16)) for hd in heads]
        for hd in heads:
            state_ref[hd] = state[hd] * gch_ref[hd] + kv[hd]
        for hd in heads:
            for r0 in range(0, C, RET_NORM_ROWS):
                rs = slice(r0, r0 + RET_NORM_ROWS)
                out_rows = slice(rows.start + r0, rows.start + r0 + RET_NORM_ROWS)
                o = inner[hd][rs] + cross[hd][rs] * xi_ref[hd, rs, :]
                mu = jnp.mean(o, axis=-1, keepdims=True)
                oc = o - mu
                var = jnp.mean(oc * oc, axis=-1, keepdims=True)
                on = oc * lax.rsqrt(var + NORM_EPS)
                gate = proj_ref[out_rows, g_off + hd * dv:g_off + (hd + 1) * dv]
                go_ref[out_rows, hd * dv:(hd + 1) * dv] = (
                    gate * _sigmoid(gate) * on).astype(BF16)

    def finish(rows):
        y = _dot(go_ref[rows, :], wout_ref[...])
        o_ref[0, rows, :] = x_ref[0, rows, :] + _gated_rmsnorm(y, g_ref[1:2, :], gate_mod)

    chunks = [slice(c * C, (c + 1) * C) for c in range(tm // C)]
    for rows in chunks:
        project(rows)
    for rows in chunks:
        core(rows)
    for rows in chunks:
        finish(rows)


def _retention_tables():
    H, C, dv = RET_HEADS, RET_KERNEL_CHUNK, RET_V_DIM
    log_gamma = jnp.log(1.0 - 2.0 ** (-5.0 - jnp.arange(H, dtype=F32)))
    idx = jnp.arange(C, dtype=F32)
    diff = idx[:, None] - idx[None, :]
    dmask = jnp.where(diff[None] >= 0,
                      jnp.exp(jnp.maximum(diff, 0.0)[None] * log_gamma[:, None, None]), 0.0)
    xi = jnp.exp((idx[None, :] + 1.0) * log_gamma[:, None])
    zeta = jnp.exp((C - 1.0 - idx[None, :]) * log_gamma[:, None])
    g_chunk = jnp.exp(C * log_gamma)
    xi_b = jnp.broadcast_to(xi[:, :, None], (H, C, dv))
    zeta_b = jnp.broadcast_to(zeta[:, :, None], (H, C, dv))
    return dmask, xi_b, zeta_b, g_chunk


def _rotary_tables(S):
    half = RET_QK_DIM // 2
    inv = ROPE_BASE ** (-jnp.arange(half, dtype=F32) / half)
    hi = (jnp.arange(S // ROT_SPLIT) * ROT_SPLIT).astype(F32)[:, None, None] * inv
    lo = jnp.arange(ROT_SPLIT).astype(F32)[None, :, None] * inv
    cos = jnp.cos(hi) * jnp.cos(lo) - jnp.sin(hi) * jnp.sin(lo)
    sin = jnp.sin(hi) * jnp.cos(lo) + jnp.cos(hi) * jnp.sin(lo)
    return cos.reshape(S, half), sin.reshape(S, half)


def _retention_layer(x, mod, g, w_in, w_out, cos, sin, tables, l, li, tm):
    B, S, Dm = x.shape
    H, dk, dv, C = RET_HEADS, RET_QK_DIM, RET_V_DIM, RET_KERNEL_CHUNK
    dmask, xi_b, zeta_b, g_chunk = tables
    row_spec = pl.BlockSpec((1, tm, Dm), lambda b, i: (b, i, 0))
    pos_spec = pl.BlockSpec((tm, dk // 2), lambda b, i: (i, 0))
    return pl.pallas_call(
        functools.partial(_ret_kernel, tm=tm, layer=li),
        grid=(B, S // tm),
        in_specs=[
            row_spec,
            _mod_spec(mod, l),
            _resident_layer(g, l),
            pl.BlockSpec(memory_space=pl.ANY),
            pl.BlockSpec(memory_space=pl.ANY),
            pos_spec,
            pos_spec,
            _resident((H, C, C)),
            _resident((H, C, dv)),
            _resident((H, C, dv)),
            pl.BlockSpec(memory_space=pltpu.SMEM),
        ],
        out_specs=row_spec,
        out_shape=jax.ShapeDtypeStruct(x.shape, F32),
        scratch_shapes=[
            pltpu.VMEM((tm, w_in.shape[-1]), F32),
            pltpu.VMEM((tm, H * dv), BF16),
            pltpu.VMEM((H, dk, dv), F32),
            pltpu.VMEM(w_in.shape[1:], BF16),
            pltpu.VMEM(w_out.shape[1:], BF16),
            pltpu.VMEM((2, RET_IN_STAGE_ROWS, w_in.shape[-1]), F32),
            pltpu.VMEM((2, RET_OUT_STAGE_ROWS, w_out.shape[-1]), F32),
            pltpu.SemaphoreType.DMA((2,)),
            pltpu.SemaphoreType.DMA((2,)),
        ],
        compiler_params=_params(),
        name="retention_sublayer",
    )(x, mod, g, w_in, w_out, cos, sin, dmask, xi_b, zeta_b, g_chunk)


def _kv_kernel(x_ref, mod_ref, g_ref, wk_ref, wvt_ref, k_ref, vt_ref):
    h = _adaln(x_ref[0], g_ref[...], mod_ref[0, 0:1, :], mod_ref[0, 1:2, :]).astype(BF16)
    k_ref[0] = _dot(h, wk_ref[...]).astype(BF16)
    vt_ref[0] = _dot_nt(wvt_ref[...], h).astype(BF16)


def _shared_kv(x, kv_mod, g, wk_pad, wv_t, tm):
    B, S, Dm = x.shape
    nk, nv = wk_pad.shape[1], wv_t.shape[0]
    return pl.pallas_call(
        _kv_kernel,
        grid=(B, S // tm),
        in_specs=[
            pl.BlockSpec((1, tm, Dm), lambda b, i: (b, i, 0)),
            pl.BlockSpec((1, 2, Dm), lambda b, i: (b, 0, 0)),
            _resident((1, Dm)),
            _resident(wk_pad.shape),
            _resident(wv_t.shape),
        ],
        out_specs=[pl.BlockSpec((1, tm, nk), lambda b, i: (b, i, 0)),
                   pl.BlockSpec((1, nv, tm), lambda b, i: (b, 0, i))],
        out_shape=[jax.ShapeDtypeStruct((B, S, nk), BF16),
                   jax.ShapeDtypeStruct((B, nv, S), BF16)],
        compiler_params=_params(),
        name="shared_kv",
    )(x, kv_mod, g, wk_pad, wv_t)


def _bias_kernel(table_ref, bucket_ref, win_ref, o_ref):
    bucket = bucket_ref[...]
    in_window = win_ref[...] != 0
    for hd in range(SWA_HEADS):
        acc = jnp.zeros(bucket.shape, F32)
        for b in range(REL_BUCKETS):
            acc = jnp.where(bucket == b, table_ref[b, hd], acc)
        o_ref[hd] = jnp.where(in_window, acc * LOG2_E, MASK_NEG)


def _bias_band_t(rel_bias):
    i = np.arange(BLOCK)[None, :]
    j = np.arange(2 * BLOCK)[:, None]
    dist = i + BLOCK - j
    n = np.maximum(dist, 0)
    max_exact = REL_BUCKETS // 2
    large = max_exact + (np.log(np.maximum(n, 1).astype(np.float32) / max_exact)
                         / math.log(REL_MAX_DIST / max_exact)
                         * (REL_BUCKETS - max_exact)).astype(np.int32)
    large = np.minimum(large, REL_BUCKETS - 1)
    bucket = np.where(n < max_exact, n, large).astype(np.int32)
    win = ((dist >= 0) & (dist < WINDOW)).astype(np.int32)
    full = pl.BlockSpec((2 * BLOCK, BLOCK), lambda i: (0, 0))
    return pl.pallas_call(
        _bias_kernel,
        grid=(1,),
        in_specs=[pl.BlockSpec(memory_space=pltpu.SMEM), full, full],
        out_specs=pl.BlockSpec((SWA_HEADS, 2 * BLOCK, BLOCK), lambda i: (0, 0, 0)),
        out_shape=jax.ShapeDtypeStruct((SWA_HEADS, 2 * BLOCK, BLOCK), F32),
        name="rel_bias_band",
    )(rel_bias, jnp.asarray(bucket), jnp.asarray(win))


def _swa_kernel(x_ref, mod_ref, g_ref, wqt_ref, wo_ref, kc_ref, kp_ref, vtc_ref, vtp_ref,
                bias_ref, sink_ref, o_ref, qt_ref, att_ref, *, tm, sub):
    KV, G, hd, KL = SWA_KV_HEADS, SWA_GROUP, SWA_HEAD_DIM, K_PAD_LANES
    n_q = SWA_HEADS * hd
    first_tile = pl.program_id(1) == 0
    shift, scale, gate_mod = mod_ref[0, 0:1, :], mod_ref[0, 1:2, :], mod_ref[0, 2:3, :]

    qt_ref[n_q:n_q + hd, :] = jnp.zeros((hd, tm), BF16)
    key_row = lax.broadcasted_iota(jnp.int32, (2 * BLOCK, GQ_ROWS), 0)
    edge_mask = jnp.where(jnp.logical_and(first_tile, key_row < BLOCK), MASK_NEG, 0.0)

    def scores(j, kh):
        cols = slice(j * BLOCK, (j + 1) * BLOCK)
        klanes = slice(kh * KL, (kh + 1) * KL)
        if j == 0:
            keys = jnp.concatenate([kp_ref[0, :, klanes], kc_ref[0, cols, klanes]], axis=0)
        else:
            keys = kc_ref[0, (j - 1) * BLOCK:(j + 1) * BLOCK, klanes]
        q_cat = jnp.concatenate(
            [qt_ref[(kh * G + g) * hd:(kh * G + g) * hd + KL, cols] for g in range(G)],
            axis=1)
        s = _dot(keys, q_cat) + bias_ref[kh]
        return s + edge_mask if j == 0 else s

    def weighted_values(j, kh, e, inv):
        cols = slice(j * BLOCK, (j + 1) * BLOCK)
        vrows = slice(kh * hd, (kh + 1) * hd)
        if j == 0:
            vals_t = jnp.concatenate([vtp_ref[0, vrows, :], vtc_ref[0, vrows, cols]], axis=1)
        else:
            vals_t = vtc_ref[0, vrows, (j - 1) * BLOCK:(j + 1) * BLOCK]
        o_t = _dot(vals_t, e.astype(BF16)) * inv
        for g in range(G):
            att_ref[(kh * G + g) * hd:(kh * G + g + 1) * hd, cols] = (
                o_t[:, g * BLOCK:(g + 1) * BLOCK].astype(BF16))

    def slabs_of(s):
        nb = sub // BLOCK
        return [(j, kh) for j in range(s * nb, (s + 1) * nb) for kh in range(KV)]

    def softmax_values(s, sc):
        slabs = slabs_of(s)
        sinks = [sink_ref[kh] * LOG2_E for _, kh in slabs]
        m = [jnp.maximum(jnp.max(si, axis=0, keepdims=True), sk) for si, sk in zip(sc, sinks)]
        e = [jnp.exp2(si - mi) for si, mi in zip(sc, m)]
        inv = [1.0 / (jnp.sum(ei, axis=0, keepdims=True) + jnp.exp2(sk - mi))
               for ei, mi, sk in zip(e, m, sinks)]
        for ei, ii, (j, kh) in zip(e, inv, slabs):
            weighted_values(j, kh, ei, ii)

    def project_q(rows):
        h = _adaln(x_ref[0, rows, :], g_ref[0:1, :], shift, scale).astype(BF16)
        qt_ref[0:n_q, rows] = (_dot_nt(wqt_ref[...], h) * (hd ** -0.5 * LOG2_E)).astype(BF16)

    def project_o(rows):
        y = _dot_tn(att_ref[:, rows], wo_ref[...])
        o_ref[0, rows, :] = x_ref[0, rows, :] + _gated_rmsnorm(y, g_ref[1:2, :], gate_mod)

    subs = [slice(s * sub, (s + 1) * sub) for s in range(tm // sub)]
    project_q(subs[0])
    for s in range(len(subs)):
        sc = [scores(j, kh) for j, kh in slabs_of(s)]
        if s + 1 < len(subs):
            project_q(subs[s + 1])
        if s > 0:
            project_o(subs[s - 1])
        softmax_values(s, sc)
    project_o(subs[-1])


def _swa_layer(x, mod, g, w_q_t, w_o, k_pad, v_t, bias_t, sink_rows, l, li, tm, sub):
    B, S, Dm = x.shape
    nk, nv = k_pad.shape[-1], v_t.shape[1]
    n_q = w_q_t.shape[1]
    nb = tm // BLOCK
    row_spec = pl.BlockSpec((1, tm, Dm), lambda b, i: (b, i, 0))
    prev_blk = lambda i: jnp.maximum(i * nb - 1, 0)
    return pl.pallas_call(
        functools.partial(_swa_kernel, tm=tm, sub=sub),
        grid=(B, S // tm),
        in_specs=[
            row_spec,
            _mod_spec(mod, l),
            _resident_layer(g, l),
            _resident_layer(w_q_t, li),
            _resident_layer(w_o, li),
            pl.BlockSpec((1, tm, nk), lambda b, i: (b, i, 0)),
            pl.BlockSpec((1, BLOCK, nk), lambda b, i: (b, prev_blk(i), 0)),
            pl.BlockSpec((1, nv, tm), lambda b, i: (b, 0, i)),
            pl.BlockSpec((1, nv, BLOCK), lambda b, i: (b, 0, prev_blk(i))),
            _resident(bias_t.shape),
            _resident_layer(sink_rows, li),
        ],
        out_specs=row_spec,
        out_shape=jax.ShapeDtypeStruct(x.shape, F32),
        scratch_shapes=[
            pltpu.VMEM((n_q + SWA_HEAD_DIM, tm), BF16),
            pltpu.VMEM((n_q, tm), BF16),
        ],
        compiler_params=_params(),
        name="swa_sublayer",
    )(x, mod, g, w_q_t, w_o, k_pad, k_pad, v_t, v_t, bias_t, sink_rows)


def _kv_weights(kv_w):
    Dm = kv_w.shape[0]
    n = SWA_KV_HEADS * SWA_HEAD_DIM
    wk = kv_w[:, :n].reshape(Dm, SWA_KV_HEADS, SWA_HEAD_DIM)
    wk = jnp.pad(wk, ((0, 0), (0, 0), (0, K_PAD_LANES - SWA_HEAD_DIM)))
    return wk.reshape(Dm, SWA_KV_HEADS * K_PAD_LANES).astype(BF16), kv_w[:, n:].T.astype(BF16)


class _Tiles(NamedTuple):
    ffn: int
    ffn_sub: int
    retention: int
    swa: int
    swa_sub: int
    kv: int


def _tile_plan(S):
    sub = V7X_MXU_DIM
    plan = _Tiles(ffn=2 * sub, ffn_sub=sub, retention=2 * RET_KERNEL_CHUNK,
                  swa=4 * sub, swa_sub=2 * sub, kv=4 * sub)
    assert all(S % t == 0 for t in plan), (S, plan)
    return plan


def kernel(x, c, norm_g, ada_w, ada_b, ret_w_in, ret_w_out, kv_norm_g, kv_ada_w, kv_ada_b, kv_w,
           swa_w_q, swa_w_o, swa_sinks, rel_bias, ffn_w_up, ffn_conv_w, ffn_conv_b, ffn_w_down):
    B, S, Dm = x.shape
    assert B <= V7X_SUBLANES and Dm == D_MODEL, x.shape
    tiles = _tile_plan(S)

    c_pad = jnp.zeros((V7X_SUBLANES, Dm), F32).at[:B].set(c)
    mod = _modulation(c_pad, ada_w, ada_b[:, None, :], MOD_COL_TILE)
    mod = mod[:, :B].reshape(DEPTH, B, 6, Dm)
    kv_mod = _modulation(c_pad, kv_ada_w[None], kv_ada_b[None, None, :], MOD_COL_TILE)
    kv_mod = kv_mod[0, :B].reshape(B, 2, Dm)

    cos, sin = _rotary_tables(S)
    tables = _retention_tables()
    bias_t = _bias_band_t(rel_bias).reshape(SWA_KV_HEADS, SWA_GROUP, 2 * BLOCK, BLOCK)
    bias_t = bias_t.transpose(0, 2, 1, 3).reshape(SWA_KV_HEADS, 2 * BLOCK, GQ_ROWS)
    sink_rows = jnp.repeat(swa_sinks.astype(F32), BLOCK, axis=1).reshape(
        swa_sinks.shape[0], SWA_KV_HEADS, 1, GQ_ROWS)

    swa_w_q_t, swa_w_o = swa_w_q.transpose(0, 2, 1).astype(BF16), swa_w_o.astype(BF16)
    ffn_conv_b = ffn_conv_b[:, None, :]

    k_pad = v_t = None
    for l in range(DEPTH):
        if l < N_A_LAYERS:
            x = _retention_layer(x, mod, norm_g, ret_w_in, ret_w_out, cos, sin, tables, l, l,
                                 tiles.retention)
        else:
            x = _swa_layer(x, mod, norm_g, swa_w_q_t, swa_w_o, k_pad, v_t, bias_t, sink_rows,
                           l, l - N_A_LAYERS, tiles.swa, tiles.swa_sub)
        x = _ffn_layer(x, mod, norm_g, ffn_w_up, ffn_conv_w, ffn_conv_b, ffn_w_down,
                       l, tiles.ffn, tiles.ffn_sub)
        if l == N_A_LAYERS - 1:
            wk_pad, wv_t = _kv_weights(kv_w)
            k_pad, v_t = _shared_kv(x, kv_mod, kv_norm_g[None, :], wk_pad, wv_t, tiles.kv)
    return x
```

```python
import functools
import math
from typing import NamedTuple

import jax
import jax.numpy as jnp
import numpy as np
from jax import lax
from jax.experimental import pallas as pl
from jax.experimental.pallas import tpu as pltpu

D_MODEL = 1024
DEPTH = 4
N_A_LAYERS = DEPTH // 2
RET_HEADS = 4
RET_QK_DIM = D_MODEL // RET_HEADS
RET_V_DIM = 2 * D_MODEL // RET_HEADS
ROPE_BASE = 10000.0
SWA_HEADS = 16
SWA_KV_HEADS = 4
SWA_GROUP = SWA_HEADS // SWA_KV_HEADS
SWA_HEAD_DIM = 64
WINDOW = 128
BLOCK = WINDOW
REL_BUCKETS = 32
REL_MAX_DIST = 128
D_FF = 2816
NORM_EPS = 1e-6

V7X_SUBLANES = 8
V7X_LANES = 128
V7X_MXU_DIM = 256
V7X_VMEM_BYTES = 64 * 1024 * 1024
V7X_VMEM_LIMIT_BYTES = V7X_VMEM_BYTES - 6 * 1024 * 1024
MOD_COL_TILE = 2048

MASK_NEG = -1e30
LOG2_E = math.log2(math.e)
BF16 = jnp.bfloat16
F32 = jnp.float32

RET_KERNEL_CHUNK = V7X_MXU_DIM
RET_NORM_ROWS = 32
FFN_COL_CHUNK = V7X_MXU_DIM
FFN_UP_STAGE_ROWS, FFN_DOWN_STAGE_ROWS = 128, 704
RET_IN_STAGE_ROWS, RET_OUT_STAGE_ROWS = 64, 256
ROT_SPLIT = 128
GQ_ROWS = SWA_GROUP * BLOCK
K_PAD_LANES = V7X_LANES
SUM_ROWS = 16


def _dot(a, b):
    return jnp.dot(a, b, preferred_element_type=F32)


def _dot_nt(a, b):
    return lax.dot_general(a, b, (((1,), (1,)), ((), ())), preferred_element_type=F32)


def _dot_tn(a, b):
    return lax.dot_general(a, b, (((0,), (0,)), ((), ())), preferred_element_type=F32)


def _sigmoid(x):
    return 1.0 / (1.0 + jnp.exp(-x))


def _adaln(x, g, shift, scale):
    ms = jnp.mean(x * x, axis=-1, keepdims=True)
    return x * lax.rsqrt(ms + NORM_EPS) * (g * (1.0 + scale)) + shift


def _gated_rmsnorm(y, g, gate):
    ms = jnp.mean(y * y, axis=-1, keepdims=True)
    return y * lax.rsqrt(ms + NORM_EPS) * (g * (1.0 + gate))


def _resident(shape):
    zeros = (0,) * len(shape)
    return pl.BlockSpec(shape, lambda b, i: zeros, pipeline_mode=pl.Buffered(1))


class _WeightLoad(NamedTuple):
    hbm: object
    bf16: object
    stage: object
    sem: object


def _load_weights_bf16(loads):
    def rows(ld):
        return ld.stage.shape[1]

    def n_chunks(ld):
        return ld.bf16.shape[0] // rows(ld)

    def chunk_copy(ld, k):
        return pltpu.make_async_copy(
            ld.hbm.at[pl.ds(k * rows(ld), rows(ld)), :], ld.stage.at[k % 2], ld.sem.at[k % 2])

    for ld in loads:
        chunk_copy(ld, 0).start()
    for k in range(max(n_chunks(ld) for ld in loads)):
        for ld in loads:
            if k < n_chunks(ld):
                if k + 1 < n_chunks(ld):
                    chunk_copy(ld, k + 1).start()
                chunk_copy(ld, k).wait()
                ld.bf16[k * rows(ld):(k + 1) * rows(ld), :] = ld.stage[k % 2].astype(BF16)


def _resident_layer(stacked, l):
    tail = (0,) * (stacked.ndim - 1)
    return pl.BlockSpec((None,) + stacked.shape[1:], lambda b, i: (l,) + tail,
                        pipeline_mode=pl.Buffered(1))


def _mod_spec(mod, l):
    return pl.BlockSpec((None, 1) + mod.shape[2:], lambda b, i: (l, b, 0, 0))


def _params():
    return pltpu.CompilerParams(
        dimension_semantics=("arbitrary", "arbitrary"),
        vmem_limit_bytes=V7X_VMEM_LIMIT_BYTES)


def _mod_kernel(c_ref, w_ref, b_ref, o_ref):
    c = c_ref[...]
    ca = (c * _sigmoid(c)).astype(BF16)
    o_ref[0] = _dot(ca, w_ref[0].astype(BF16)) + b_ref[0]


def _modulation(c_pad, w, b, tn):
    L, Dm, N = w.shape
    return pl.pallas_call(
        _mod_kernel,
        grid=(L, N // tn),
        in_specs=[
            pl.BlockSpec((V7X_SUBLANES, Dm), lambda l, j: (0, 0)),
            pl.BlockSpec((1, Dm, tn), lambda l, j: (l, 0, j)),
            pl.BlockSpec((1, 1, tn), lambda l, j: (l, 0, j)),
        ],
        out_specs=pl.BlockSpec((1, V7X_SUBLANES, tn), lambda l, j: (l, 0, j)),
        out_shape=jax.ShapeDtypeStruct((L, V7X_SUBLANES, N), F32),
        compiler_params=_params(),
        name="adaln_modulation",
    )(c_pad, w, b)


def _ffn_kernel(x_ref, mod_ref, g_ref, wup_hbm, cw_ref, cb_ref, wdn_hbm, o_ref,
                u_ref, act_ref, carry_ref, wup_ref, wdn_ref, up_stage, dn_stage, up_sem, dn_sem,
                *, tm, sub, layer):
    F = D_FF
    FC = FFN_COL_CHUNK

    @pl.when(jnp.logical_and(pl.program_id(0) == 0, pl.program_id(1) == 0))
    def _():
        _load_weights_bf16([_WeightLoad(wup_hbm.at[layer], wup_ref, up_stage, up_sem),
                            _WeightLoad(wdn_hbm.at[layer], wdn_ref, dn_stage, dn_sem)])

    @pl.when(pl.program_id(1) == 0)
    def _():
        carry_ref[...] = jnp.zeros_like(carry_ref)

    shift, scale, gate_mod = mod_ref[0, 3:4, :], mod_ref[0, 4:5, :], mod_ref[0, 5:6, :]
    row = lax.broadcasted_iota(jnp.int32, (sub, FC), 0)

    def conv(slot, cols):
        u = u_ref[slot, :, cols]
        prev1 = carry_ref[V7X_SUBLANES - 1:V7X_SUBLANES, cols]
        prev2 = carry_ref[V7X_SUBLANES - 2:V7X_SUBLANES - 1, cols]
        u1 = jnp.where(row == 0, prev1, pltpu.roll(u, 1, 0))
        u2 = jnp.where(row == 0, prev2, jnp.where(row == 1, prev1, pltpu.roll(u, 2, 0)))
        return (cb_ref[0:1, cols] + cw_ref[0:1, cols] * u2 + cw_ref[1:2, cols] * u1
                + cw_ref[2:3, cols] * u)

    def up(s):
        rows = slice(s * sub, (s + 1) * sub)
        h = _adaln(x_ref[0, rows, :], g_ref[2:3, :], shift, scale).astype(BF16)
        u_ref[s % 2] = _dot(h, wup_ref[...])

    def activate(s):
        slot = s % 2
        for c in range(F // FC):
            gate = conv(slot, slice(c * FC, (c + 1) * FC))
            val = conv(slot, slice(F + c * FC, F + (c + 1) * FC))
            act_ref[slot, :, c * FC:(c + 1) * FC] = (gate * _sigmoid(gate) * val).astype(BF16)
        carry_ref[...] = u_ref[slot, sub - V7X_SUBLANES:sub, :]

    def down(s):
        rows = slice(s * sub, (s + 1) * sub)
        y = _dot(act_ref[s % 2], wdn_ref[...])
        o_ref[0, rows, :] = x_ref[0, rows, :] + _gated_rmsnorm(y, g_ref[3:4, :], gate_mod)

    n_sub = tm // sub
    up(0)
    for s in range(n_sub):
        if s + 1 < n_sub:
            up(s + 1)
        activate(s)
        down(s)


def _ffn_layer(x, mod, g, w_up, conv_w, conv_b, w_down, l, tm, sub):
    B, S, Dm = x.shape
    F2 = w_up.shape[-1]
    row_spec = pl.BlockSpec((1, tm, Dm), lambda b, i: (b, i, 0))
    return pl.pallas_call(
        functools.partial(_ffn_kernel, tm=tm, sub=sub, layer=l),
        grid=(B, S // tm),
        in_specs=[
            row_spec,
            _mod_spec(mod, l),
            _resident_layer(g, l),
            pl.BlockSpec(memory_space=pl.ANY),
            _resident_layer(conv_w, l),
            _resident_layer(conv_b, l),
            pl.BlockSpec(memory_space=pl.ANY),
        ],
        out_specs=row_spec,
        out_shape=jax.ShapeDtypeStruct(x.shape, F32),
        scratch_shapes=[
            pltpu.VMEM((2, sub, F2), F32),
            pltpu.VMEM((2, sub, F2 // 2), BF16),
            pltpu.VMEM((V7X_SUBLANES, F2), F32),
            pltpu.VMEM((Dm, F2), BF16),
            pltpu.VMEM((F2 // 2, Dm), BF16),
            pltpu.VMEM((2, FFN_UP_STAGE_ROWS, F2), F32),
            pltpu.VMEM((2, FFN_DOWN_STAGE_ROWS, Dm), F32),
            pltpu.SemaphoreType.DMA((2,)),
            pltpu.SemaphoreType.DMA((2,)),
        ],
        compiler_params=_params(),
        name="conv_ffn_sublayer",
    )(x, mod, g, w_up, conv_w, conv_b, w_down)


def _ret_kernel(x_ref, mod_ref, g_ref, win_hbm, wout_hbm, cos_ref, sin_ref, dmask_ref,
                xi_ref, zeta_ref, gch_ref, o_ref, proj_ref, go_ref, state_ref,
                win_ref, wout_ref, in_stage, out_stage, in_sem, out_sem, *, tm, layer):
    H, dk, dv, C = RET_HEADS, RET_QK_DIM, RET_V_DIM, RET_KERNEL_CHUNK
    half = dk // 2
    k_off, v_off, g_off = H * dk, 2 * H * dk, 2 * H * dk + H * dv

    @pl.when(jnp.logical_and(pl.program_id(0) == 0, pl.program_id(1) == 0))
    def _():
        _load_weights_bf16([_WeightLoad(win_hbm.at[layer], win_ref, in_stage, in_sem),
                            _WeightLoad(wout_hbm.at[layer], wout_ref, out_stage, out_sem)])

    @pl.when(pl.program_id(1) == 0)
    def _():
        state_ref[...] = jnp.zeros_like(state_ref)

    shift, scale, gate_mod = mod_ref[0, 0:1, :], mod_ref[0, 1:2, :], mod_ref[0, 2:3, :]

    def rotary(t, cos, sin):
        t1, t2 = t[:, :half], t[:, half:]
        return jnp.concatenate([t1 * cos - t2 * sin, t2 * cos + t1 * sin], axis=-1)

    def project(rows):
        h = _adaln(x_ref[0, rows, :], g_ref[0:1, :], shift, scale).astype(BF16)
        proj_ref[rows, :] = _dot(h, win_ref[...])

    heads = range(H)

    def core(rows):
        cos, sin = cos_ref[rows, :], sin_ref[rows, :]
        q = [rotary(proj_ref[rows, hd * dk:(hd + 1) * dk], cos, sin).astype(BF16) for hd in heads]
        k = [(rotary(proj_ref[rows, k_off + hd * dk:k_off + (hd + 1) * dk], cos, sin)
              * (dk ** -0.5)).astype(BF16) for hd in heads]
        v = [proj_ref[rows, v_off + hd * dv:v_off + (hd + 1) * dv] for hd in heads]
        s = [_dot_nt(q[hd], k[hd]) for hd in heads]
        s = [(s[hd] * dmask_ref[hd]).astype(BF16) for hd in heads]
        inner = [_dot(s[hd], v[hd].astype(BF16)) for hd in heads]
        state = [state_ref[hd] for hd in heads]
        cross = [_dot(q[hd], state[hd].astype(BF16)) for hd in heads]
        kv = [_dot_tn(k[hd], (v[hd] * zeta_ref[hd]).astype(BF16)) for hd in heads]
        for hd in heads:
            state_ref[hd] = state[hd] * gch_ref[hd] + kv[hd]
        for hd in heads:
            for r0 in range(0, C, RET_NORM_ROWS):
                rs = slice(r0, r0 + RET_NORM_ROWS)
                out_rows = slice(rows.start + r0, rows.start + r0 + RET_NORM_ROWS)
                o = inner[hd][rs] + cross[hd][rs] * xi_ref[hd, rs, :]
                mu = jnp.mean(o, axis=-1, keepdims=True)
                oc = o - mu
                var = jnp.mean(oc * oc, axis=-1, keepdims=True)
                on = oc * lax.rsqrt(var + NORM_EPS)
                gate = proj_ref[out_rows, g_off + hd * dv:g_off + (hd + 1) * dv]
                go_ref[out_rows, hd * dv:(hd + 1) * dv] = (
                    gate * _sigmoid(gate) * on).astype(BF16)

    def finish(rows):
        y = _dot(go_ref[rows, :], wout_ref[...])
        o_ref[0, rows, :] = x_ref[0, rows, :] + _gated_rmsnorm(y, g_ref[1:2, :], gate_mod)

    chunks = [slice(c * C, (c + 1) * C) for c in range(tm // C)]
    for rows in chunks:
        project(rows)
    for rows in chunks:
        core(rows)
    for rows in chunks:
        finish(rows)


def _retention_tables():
    H, C, dv = RET_HEADS, RET_KERNEL_CHUNK, RET_V_DIM
    log_gamma = jnp.log(1.0 - 2.0 ** (-5.0 - jnp.arange(H, dtype=F32)))
    idx = jnp.arange(C, dtype=F32)
    diff = idx[:, None] - idx[None, :]
    dmask = jnp.where(diff[None] >= 0,
                      jnp.exp(jnp.maximum(diff, 0.0)[None] * log_gamma[:, None, None]), 0.0)
    xi = jnp.exp((idx[None, :] + 1.0) * log_gamma[:, None])
    zeta = jnp.exp((C - 1.0 - idx[None, :]) * log_gamma[:, None])
    g_chunk = jnp.exp(C * log_gamma)
    xi_b = jnp.broadcast_to(xi[:, :, None], (H, C, dv))
    zeta_b = jnp.broadcast_to(zeta[:, :, None], (H, C, dv))
    return dmask, xi_b, zeta_b, g_chunk


def _rotary_tables(S):
    half = RET_QK_DIM // 2
    inv = ROPE_BASE ** (-jnp.arange(half, dtype=F32) / half)
    hi = (jnp.arange(S // ROT_SPLIT) * ROT_SPLIT).astype(F32)[:, None, None] * inv
    lo = jnp.arange(ROT_SPLIT).astype(F32)[None, :, None] * inv
    cos = jnp.cos(hi) * jnp.cos(lo) - jnp.sin(hi) * jnp.sin(lo)
    sin = jnp.sin(hi) * jnp.cos(lo) + jnp.cos(hi) * jnp.sin(lo)
    return cos.reshape(S, half), sin.reshape(S, half)


def _retention_layer(x, mod, g, w_in, w_out, cos, sin, tables, l, li, tm):
    B, S, Dm = x.shape
    H, dk, dv, C = RET_HEADS, RET_QK_DIM, RET_V_DIM, RET_KERNEL_CHUNK
    dmask, xi_b, zeta_b, g_chunk = tables
    row_spec = pl.BlockSpec((1, tm, Dm), lambda b, i: (b, i, 0))
    pos_spec = pl.BlockSpec((tm, dk // 2), lambda b, i: (i, 0))
    return pl.pallas_call(
        functools.partial(_ret_kernel, tm=tm, layer=li),
        grid=(B, S // tm),
        in_specs=[
            row_spec,
            _mod_spec(mod, l),
            _resident_layer(g, l),
            pl.BlockSpec(memory_space=pl.ANY),
            pl.BlockSpec(memory_space=pl.ANY),
            pos_spec,
            pos_spec,
            _resident((H, C, C)),
            _resident((H, C, dv)),
            _resident((H, C, dv)),
            pl.BlockSpec(memory_space=pltpu.SMEM),
        ],
        out_specs=row_spec,
        out_shape=jax.ShapeDtypeStruct(x.shape, F32),
        scratch_shapes=[
            pltpu.VMEM((tm, w_in.shape[-1]), F32),
            pltpu.VMEM((tm, H * dv), BF16),
            pltpu.VMEM((H, dk, dv), F32),
            pltpu.VMEM(w_in.shape[1:], BF16),
            pltpu.VMEM(w_out.shape[1:], BF16),
            pltpu.VMEM((2, RET_IN_STAGE_ROWS, w_in.shape[-1]), F32),
            pltpu.VMEM((2, RET_OUT_STAGE_ROWS, w_out.shape[-1]), F32),
            pltpu.SemaphoreType.DMA((2,)),
            pltpu.SemaphoreType.DMA((2,)),
        ],
        compiler_params=_params(),
        name="retention_sublayer",
    )(x, mod, g, w_in, w_out, cos, sin, dmask, xi_b, zeta_b, g_chunk)


def _kv_kernel(x_ref, mod_ref, g_ref, wk_ref, wvt_ref, k_ref, vt_ref):
    h = _adaln(x_ref[0], g_ref[...], mod_ref[0, 0:1, :], mod_ref[0, 1:2, :]).astype(BF16)
    k_ref[0] = _dot(h, wk_ref[...]).astype(BF16)
    vt_ref[0] = _dot_nt(wvt_ref[...], h).astype(BF16)


def _shared_kv(x, kv_mod, g, wk_pad, wv_t, tm):
    B, S, Dm = x.shape
    nk, nv = wk_pad.shape[1], wv_t.shape[0]
    return pl.pallas_call(
        _kv_kernel,
        grid=(B, S // tm),
        in_specs=[
            pl.BlockSpec((1, tm, Dm), lambda b, i: (b, i, 0)),
            pl.BlockSpec((1, 2, Dm), lambda b, i: (b, 0, 0)),
            _resident((1, Dm)),
            _resident(wk_pad.shape),
            _resident(wv_t.shape),
        ],
        out_specs=[pl.BlockSpec((1, tm, nk), lambda b, i: (b, i, 0)),
                   pl.BlockSpec((1, nv, tm), lambda b, i: (b, 0, i))],
        out_shape=[jax.ShapeDtypeStruct((B, S, nk), BF16),
                   jax.ShapeDtypeStruct((B, nv, S), BF16)],
        compiler_params=_params(),
        name="shared_kv",
    )(x, kv_mod, g, wk_pad, wv_t)


def _bias_kernel(table_ref, bucket_ref, win_ref, o_ref):
    bucket = bucket_ref[...]
    in_window = win_ref[...] != 0
    for hd in range(SWA_HEADS):
        acc = jnp.zeros(bucket.shape, F32)
        for b in range(REL_BUCKETS):
            acc = jnp.where(bucket == b, table_ref[b, hd], acc)
        o_ref[hd] = jnp.where(in_window, acc * LOG2_E, MASK_NEG)


def _bias_band_t(rel_bias):
    i = np.arange(BLOCK)[None, :]
    j = np.arange(2 * BLOCK)[:, None]
    dist = i + BLOCK - j
    n = np.maximum(dist, 0)
    max_exact = REL_BUCKETS // 2
    large = max_exact + (np.log(np.maximum(n, 1).astype(np.float32) / max_exact)
                         / math.log(REL_MAX_DIST / max_exact)
                         * (REL_BUCKETS - max_exact)).astype(np.int32)
    large = np.minimum(large, REL_BUCKETS - 1)
    bucket = np.where(n < max_exact, n, large).astype(np.int32)
    win = ((dist >= 0) & (dist < WINDOW)).astype(np.int32)
    full = pl.BlockSpec((2 * BLOCK, BLOCK), lambda i: (0, 0))
    return pl.pallas_call(
        _bias_kernel,
        grid=(1,),
        in_specs=[pl.BlockSpec(memory_space=pltpu.SMEM), full, full],
        out_specs=pl.BlockSpec((SWA_HEADS, 2 * BLOCK, BLOCK), lambda i: (0, 0, 0)),
        out_shape=jax.ShapeDtypeStruct((SWA_HEADS, 2 * BLOCK, BLOCK), F32),
        name="rel_bias_band",
    )(rel_bias, jnp.asarray(bucket), jnp.asarray(win))


def _swa_kernel(x_ref, mod_ref, g_ref, wqt_ref, wo_ref, kc_ref, kp_ref, vtc_ref, vtp_ref,
                bias_ref, sink_ref, o_ref, qt_ref, att_ref, *, tm, sub):
    KV, G, hd, KL = SWA_KV_HEADS, SWA_GROUP, SWA_HEAD_DIM, K_PAD_LANES
    n_q = SWA_HEADS * hd
    first_tile = pl.program_id(1) == 0
    shift, scale, gate_mod = mod_ref[0, 0:1, :], mod_ref[0, 1:2, :], mod_ref[0, 2:3, :]

    qt_ref[n_q:n_q + hd, :] = jnp.zeros((hd, tm), BF16)
    key_row = lax.broadcasted_iota(jnp.int32, (2 * BLOCK, GQ_ROWS), 0)
    edge_mask = jnp.where(jnp.logical_and(first_tile, key_row < BLOCK), MASK_NEG, 0.0)

    def scores(j, kh):
        cols = slice(j * BLOCK, (j + 1) * BLOCK)
        klanes = slice(kh * KL, (kh + 1) * KL)
        if j == 0:
            keys = jnp.concatenate([kp_ref[0, :, klanes], kc_ref[0, cols, klanes]], axis=0)
        else:
            keys = kc_ref[0, (j - 1) * BLOCK:(j + 1) * BLOCK, klanes]
        q_cat = jnp.concatenate(
            [qt_ref[(kh * G + g) * hd:(kh * G + g) * hd + KL, cols] for g in range(G)],
            axis=1)
        s = _dot(keys, q_cat) + bias_ref[kh]
        return s + edge_mask if j == 0 else s

    ones_rows = jnp.ones((SUM_ROWS, 2 * BLOCK), BF16)

    def weighted_values(j, kh, e, sink_term):
        cols = slice(j * BLOCK, (j + 1) * BLOCK)
        vrows = slice(kh * hd, (kh + 1) * hd)
        if j == 0:
            vals_t = jnp.concatenate([vtp_ref[0, vrows, :], vtc_ref[0, vrows, cols]], axis=1)
        else:
            vals_t = vtc_ref[0, vrows, (j - 1) * BLOCK:(j + 1) * BLOCK]
        o_sum = _dot(jnp.concatenate([vals_t, ones_rows], axis=0), e.astype(BF16))
        o_t = o_sum[:hd] * (1.0 / (o_sum[hd:hd + 1] + sink_term))
        for g in range(G):
            att_ref[(kh * G + g) * hd:(kh * G + g + 1) * hd, cols] = (
                o_t[:, g * BLOCK:(g + 1) * BLOCK].astype(BF16))

    def slabs_of(s):
        nb = sub // BLOCK
        return [(j, kh) for j in range(s * nb, (s + 1) * nb) for kh in range(KV)]

    def softmax_values(s, sc):
        slabs = slabs_of(s)
        sinks = [sink_ref[kh] * LOG2_E for _, kh in slabs]
        m = [jnp.maximum(jnp.max(si, axis=0, keepdims=True), sk) for si, sk in zip(sc, sinks)]
        e = [jnp.exp2(si - mi) for si, mi in zip(sc, m)]
        for ei, mi, sk, (j, kh) in zip(e, m, sinks, slabs):
            weighted_values(j, kh, ei, jnp.exp2(sk - mi))

    def project_q(rows):
        h = _adaln(x_ref[0, rows, :], g_ref[0:1, :], shift, scale).astype(BF16)
        qt_ref[0:n_q, rows] = (_dot_nt(wqt_ref[...], h) * (hd ** -0.5 * LOG2_E)).astype(BF16)

    def project_o(rows):
        y = _dot_tn(att_ref[:, rows], wo_ref[...])
        o_ref[0, rows, :] = x_ref[0, rows, :] + _gated_rmsnorm(y, g_ref[1:2, :], gate_mod)

    subs = [slice(s * sub, (s + 1) * sub) for s in range(tm // sub)]
    project_q(subs[0])
    for s in range(len(subs)):
        sc = [scores(j, kh) for j, kh in slabs_of(s)]
        if s + 1 < len(subs):
            project_q(subs[s + 1])
        if s > 0:
            project_o(subs[s - 1])
        softmax_values(s, sc)
    project_o(subs[-1])


def _swa_layer(x, mod, g, w_q_t, w_o, k_pad, v_t, bias_t, sink_rows, l, li, tm, sub):
    B, S, Dm = x.shape
    nk, nv = k_pad.shape[-1], v_t.shape[1]
    n_q = w_q_t.shape[1]
    nb = tm // BLOCK
    row_spec = pl.BlockSpec((1, tm, Dm), lambda b, i: (b, i, 0))
    prev_blk = lambda i: jnp.maximum(i * nb - 1, 0)
    return pl.pallas_call(
        functools.partial(_swa_kernel, tm=tm, sub=sub),
        grid=(B, S // tm),
        in_specs=[
            row_spec,
            _mod_spec(mod, l),
            _resident_layer(g, l),
            _resident_layer(w_q_t, li),
            _resident_layer(w_o, li),
            pl.BlockSpec((1, tm, nk), lambda b, i: (b, i, 0)),
            pl.BlockSpec((1, BLOCK, nk), lambda b, i: (b, prev_blk(i), 0)),
            pl.BlockSpec((1, nv, tm), lambda b, i: (b, 0, i)),
            pl.BlockSpec((1, nv, BLOCK), lambda b, i: (b, 0, prev_blk(i))),
            _resident(bias_t.shape),
            _resident_layer(sink_rows, li),
        ],
        out_specs=row_spec,
        out_shape=jax.ShapeDtypeStruct(x.shape, F32),
        scratch_shapes=[
            pltpu.VMEM((n_q + SWA_HEAD_DIM, tm), BF16),
            pltpu.VMEM((n_q, tm), BF16),
        ],
        compiler_params=_params(),
        name="swa_sublayer",
    )(x, mod, g, w_q_t, w_o, k_pad, k_pad, v_t, v_t, bias_t, sink_rows)


def _kv_weights(kv_w):
    Dm = kv_w.shape[0]
    n = SWA_KV_HEADS * SWA_HEAD_DIM
    wk = kv_w[:, :n].reshape(Dm, SWA_KV_HEADS, SWA_HEAD_DIM)
    wk = jnp.pad(wk, ((0, 0), (0, 0), (0, K_PAD_LANES - SWA_HEAD_DIM)))
    return wk.reshape(Dm, SWA_KV_HEADS * K_PAD_LANES).astype(BF16), kv_w[:, n:].T.astype(BF16)


class _Tiles(NamedTuple):
    ffn: int
    ffn_sub: int
    retention: int
    swa: int
    swa_sub: int
    kv: int


def _tile_plan(S):
    sub = V7X_MXU_DIM
    plan = _Tiles(ffn=2 * sub, ffn_sub=sub, retention=2 * RET_KERNEL_CHUNK,
                  swa=4 * sub, swa_sub=2 * sub, kv=4 * sub)
    assert all(S % t == 0 for t in plan), (S, plan)
    return plan


def kernel(x, c, norm_g, ada_w, ada_b, ret_w_in, ret_w_out, kv_norm_g, kv_ada_w, kv_ada_b, kv_w,
           swa_w_q, swa_w_o, swa_sinks, rel_bias, ffn_w_up, ffn_conv_w, ffn_conv_b, ffn_w_down):
    B, S, Dm = x.shape
    assert B <= V7X_SUBLANES and Dm == D_MODEL, x.shape
    tiles = _tile_plan(S)

    c_pad = jnp.zeros((V7X_SUBLANES, Dm), F32).at[:B].set(c)
    mod = _modulation(c_pad, ada_w, ada_b[:, None, :], MOD_COL_TILE)
    mod = mod[:, :B].reshape(DEPTH, B, 6, Dm)
    kv_mod = _modulation(c_pad, kv_ada_w[None], kv_ada_b[None, None, :], MOD_COL_TILE)
    kv_mod = kv_mod[0, :B].reshape(B, 2, Dm)

    cos, sin = _rotary_tables(S)
    tables = _retention_tables()
    bias_t = _bias_band_t(rel_bias).reshape(SWA_KV_HEADS, SWA_GROUP, 2 * BLOCK, BLOCK)
    bias_t = bias_t.transpose(0, 2, 1, 3).reshape(SWA_KV_HEADS, 2 * BLOCK, GQ_ROWS)
    sink_rows = jnp.repeat(swa_sinks.astype(F32), BLOCK, axis=1).reshape(
        swa_sinks.shape[0], SWA_KV_HEADS, 1, GQ_ROWS)

    swa_w_q_t, swa_w_o = swa_w_q.transpose(0, 2, 1).astype(BF16), swa_w_o.astype(BF16)
    ffn_conv_b = ffn_conv_b[:, None, :]

    k_pad = v_t = None
    for l in range(DEPTH):
        if l < N_A_LAYERS:
            x = _retention_layer(x, mod, norm_g, ret_w_in, ret_w_out, cos, sin, tables, l, l,
                                 tiles.retention)
        else:
            x = _swa_layer(x, mod, norm_g, swa_w_q_t, swa_w_o, k_pad, v_t, bias_t, sink_rows,
                           l, l - N_A_LAYERS, tiles.swa, tiles.swa_sub)
        x = _ffn_layer(x, mod, norm_g, ffn_w_up, ffn_conv_w, ffn_conv_b, ffn_w_down,
                       l, tiles.ffn, tiles.ffn_sub)
        if l == N_A_LAYERS - 1:
            wk_pad, wv_t = _kv_weights(kv_w)
            k_pad, v_t = _shared_kv(x, kv_mod, kv_norm_g[None, :], wk_pad, wv_t, tiles.kv)
    return x
```
